```python
import jax
import jax.numpy as jnp
from jax import lax
import numpy as np

D_MODEL = 2048
BATCH = 4
SEQ = 8192
DEPTH = 4

MEM_LEN = 256
BRANCH_W = 512
N_BRANCHES = 3

SG_CHUNK = 128
SG_GROUPS = 4
SG_DIM = BRANCH_W // SG_GROUPS

HG_HEADS = 4
HG_DK = BRANCH_W // HG_HEADS
HG_DV = BRANCH_W // HG_HEADS
HG_CHUNK = 64
HG_EPS = 1e-6
HG_EXP_CLIP = 60.0

RW_HEAD = 64
RW_HEADS = BRANCH_W // RW_HEAD
RW_W_LORA = 96
RW_A_LORA = 96
RW_V_LORA = 64
RW_G_LORA = 256
RW_GN_EPS = 64e-5

XA_HEADS = 4
XA_HEAD_DIM = 128
XA_W = XA_HEADS * XA_HEAD_DIM

D_FF = 5632
N_EXPERTS = 8
TOP_K = 2
MOE_BLOCK = 256
N_DENSE = (DEPTH + 1) // 2
N_MOE = DEPTH // 2

DN_ALPHA = (2 * DEPTH) ** 0.25
DN_BETA = (8 * DEPTH) ** -0.25
LN_EPS = 1e-5

SG_COLS = 2 * BRANCH_W
HG_COLS = 4 * BRANCH_W
RW_COLS = 3 * BRANCH_W + RW_W_LORA + RW_A_LORA + RW_G_LORA
GATE_COLS = N_BRANCHES * D_MODEL
IN_COLS = SG_COLS + HG_COLS + RW_COLS + GATE_COLS
IN_SPLITS = (SG_COLS, SG_COLS + HG_COLS, SG_COLS + HG_COLS + RW_COLS)
RW_SPLITS = (BRANCH_W, 2 * BRANCH_W, 3 * BRANCH_W, 3 * BRANCH_W + RW_W_LORA,
             3 * BRANCH_W + RW_W_LORA + RW_A_LORA)

kernel_name = 'hybrid_gated_branch_trunk'


def layer_norm(x, g, b, eps=LN_EPS):
    xf = x.astype(jnp.float32)
    mu = jnp.mean(xf, axis=-1, keepdims=True)
    var = jnp.mean(jnp.square(xf - mu), axis=-1, keepdims=True)
    return ((xf - mu) * lax.rsqrt(var + eps) * g + b).astype(x.dtype)


def rms_norm(x, g, eps):
    xf = x.astype(jnp.float32)
    return (xf * lax.rsqrt(jnp.mean(jnp.square(xf), axis=-1, keepdims=True) + eps) * g).astype(x.dtype)


def token_shift(z):
    return jnp.pad(z, ((0, 0), (1, 0), (0, 0)))[:, :-1]


def spatial_gating_branch(z, w_s, b_s, ln_g, ln_b):
    B, L, _ = z.shape
    z = jax.nn.gelu(z)
    u, v = jnp.split(z, 2, axis=-1)
    v = layer_norm(v, ln_g, ln_b)
    vc = v.reshape(B, L // SG_CHUNK, SG_CHUNK, SG_GROUPS, SG_DIM)
    causal = jnp.tril(jnp.ones((SG_CHUNK, SG_CHUNK), dtype=bool))
    w = jnp.where(causal, w_s, 0)
    s = jnp.einsum('gts,bnsgc->bntgc', w, vc) + b_s.T[:, :, None]
    return u * s.reshape(B, L, BRANCH_W)


def chunked_gated_recurrence(q, k, v, log_f):
    B, L, H, dk = q.shape
    dv = v.shape[-1]
    n = L // HG_CHUNK

    def to_chunks(t):
        return t.reshape(B, n, HG_CHUNK, H, t.shape[-1]).transpose(1, 0, 3, 2, 4)

    qc, kc, vc, gc = (to_chunks(t) for t in (q, k, v, log_f))
    causal = jnp.tril(jnp.ones((HG_CHUNK, HG_CHUNK), dtype=bool))[:, :, None]

    def step(S, inp):
        q_, k_, v_, g_ = inp
        b = jnp.cumsum(g_, axis=2)
        o_inter = jnp.einsum('bhtk,bhkv->bhtv', q_ * jnp.exp(b), S)
        diff = b[:, :, :, None, :] - b[:, :, None, :, :]
        decay = jnp.where(causal, jnp.exp(jnp.where(causal, diff, 0.0)), 0.0)
        att = jnp.einsum('bhtk,bhsk,bhtsk->bhts', q_, k_, decay)
        o = o_inter + jnp.einsum('bhts,bhsv->bhtv', att, v_)
        b_last = b[:, :, -1:, :]
        S = jnp.exp(b_last[:, :, 0, :])[..., None] * S + jnp.einsum(
            'bhsk,bhsv->bhkv', k_ * jnp.exp(b_last - b), v_)
        return S, o

    S0 = jnp.zeros((B, H, dk, dv), jnp.float32)
    _, o = lax.scan(step, S0, (qc, kc, vc, gc))
    return o.transpose(1, 0, 3, 2, 4).reshape(B, L, H, dv)


def hgrn2_branch(zq, zf, zi, zg, lb, norm_g):
    B, L, _ = zq.shape
    f = zf.astype(jnp.float32)
    lb = lb.astype(jnp.float32)
    log_f = jax.nn.log_sigmoid(f) + jnp.log1p(lb * jnp.exp(jnp.minimum(-f, HG_EXP_CLIP)))
    log_f = jnp.minimum(log_f, 0.0)
    k = (1.0 - lb) * jax.nn.sigmoid(-f)
    q = jax.nn.silu(zq.astype(jnp.float32))
    i = zi.astype(jnp.float32)
    o = chunked_gated_recurrence(
        q.reshape(B, L, HG_HEADS, HG_DK), k.reshape(B, L, HG_HEADS, HG_DK),
        i.reshape(B, L, HG_HEADS, HG_DV), log_f.reshape(B, L, HG_HEADS, HG_DK))
    o = o.reshape(B, L, BRANCH_W) * jax.nn.sigmoid(zg.astype(jnp.float32))
    return rms_norm(o, norm_g, HG_EPS)


def rwkv7_recurrence(r, w, k, v, a, b):
    B, L, H, N = r.shape

    def step(S, inp):
        r_, w_, k_, v_, a_, b_ = inp
        sa = jnp.einsum('bhvk,bhk->bhv', S, a_)
        S = S * w_[:, :, None, :] + sa[..., None] * b_[:, :, None, :] + v_[..., None] * k_[:, :, None, :]
        return S, jnp.einsum('bhvk,bhk->bhv', S, r_)

    xs = tuple(t.astype(jnp.float32).transpose(1, 0, 2, 3) for t in (r, w, k, v, a, b))
    S0 = jnp.zeros((B, H, N, N), jnp.float32)
    _, y = lax.scan(step, S0, xs)
    return y.transpose(1, 0, 2, 3)


def rwkv7_branch(zc, mu, w0, w2, a0, a2, g2, k_k, k_a, r_k, gn_g, gn_b, v_first, v_mix):
    B, L, _ = zc.shape
    zc = zc + (token_shift(zc) - zc) * mu
    r, k, v, wl, al, gl = jnp.split(zc, RW_SPLITS, axis=-1)
    w_log = -jax.nn.softplus(-(w0 + jnp.tanh(wl) @ w2).astype(jnp.float32)) - 0.5
    decay = jnp.exp(-jnp.exp(w_log))
    a = jax.nn.sigmoid((a0 + al @ a2).astype(jnp.float32))
    g = jax.nn.sigmoid(gl) @ g2
    if v_mix is None:
        v_first = v
    else:
        v0, v1, v2 = v_mix
        v = v + (v_first - v) * jax.nn.sigmoid(v0 + (v @ v1) @ v2)

    def heads(t):
        return t.reshape(B, L, RW_HEADS, RW_HEAD)

    kk = heads(k * k_k).astype(jnp.float32)
    kk = kk / jnp.maximum(jnp.sqrt(jnp.sum(jnp.square(kk), axis=-1, keepdims=True)), 1e-12)
    k = k * (1 + (a - 1) * k_a)
    y = rwkv7_recurrence(heads(r), heads(decay), heads(k), heads(v), -kk, kk * heads(a))
    mu_y = jnp.mean(y, axis=-1, keepdims=True)
    var_y = jnp.mean(jnp.square(y - mu_y), axis=-1, keepdims=True)
    y = ((y - mu_y) * lax.rsqrt(var_y + RW_GN_EPS)).reshape(B, L, BRANCH_W) * gn_g + gn_b
    bonus = jnp.sum(heads(r) * heads(k) * r_k.reshape(RW_HEADS, RW_HEAD), axis=-1, keepdims=True)
    y = y + (bonus * heads(v)).reshape(B, L, BRANCH_W)
    return y * g, v_first


def cross_attention(x, mem, wq, wk, wv, wo):
    B, L, _ = x.shape
    M = mem.shape[1]
    q = (x @ wq).reshape(B, L, XA_HEADS, XA_HEAD_DIM)
    k = (mem @ wk).reshape(B, M, XA_HEADS, XA_HEAD_DIM)
    v = (mem @ wv).reshape(B, M, XA_HEADS, XA_HEAD_DIM)
    s = jnp.einsum('blhd,bmhd->bhlm', q, k, preferred_element_type=jnp.float32) * (XA_HEAD_DIM ** -0.5)
    p = jax.nn.softmax(s, axis=-1).astype(v.dtype)
    o = jnp.einsum('bhlm,bmhd->blhd', p, v).reshape(B, L, XA_W)
    return o @ wo


def swiglu(x, w1, w3, w2):
    return (jax.nn.silu(x @ w1) * (x @ w3)) @ w2


def moe_swiglu(x, w_router, w1, w3, w2):
    B, L, D = x.shape
    xt = x.reshape(-1, D)
    T = xt.shape[0]
    logits = (xt @ w_router).astype(jnp.float32)
    top_val, top_idx = lax.top_k(logits, TOP_K)
    gates = jax.nn.softmax(top_val, axis=-1)
    flat_e = top_idx.reshape(-1)
    flat_tok = jnp.repeat(jnp.arange(T, dtype=jnp.int32), TOP_K)
    flat_w = gates.reshape(-1)
    order = jnp.argsort(flat_e)
    e_sorted, tok_sorted, w_sorted = flat_e[order], flat_tok[order], flat_w[order]
    counts = jnp.bincount(flat_e, length=N_EXPERTS)
    padded = (counts + MOE_BLOCK - 1) // MOE_BLOCK * MOE_BLOCK
    start = jnp.cumsum(counts) - counts
    ends = jnp.cumsum(padded)
    pstart = ends - padded
    dest = pstart[e_sorted] + (jnp.arange(T * TOP_K) - start[e_sorted])
    P = -(-(T * TOP_K) // MOE_BLOCK) * MOE_BLOCK + N_EXPERTS * MOE_BLOCK
    n_blocks = P // MOE_BLOCK
    slot_tok = jnp.full((P,), T, jnp.int32).at[dest].set(tok_sorted)
    slot_w = jnp.zeros((P,), jnp.float32).at[dest].set(w_sorted)
    block_expert = jnp.minimum(
        jnp.searchsorted(ends, jnp.arange(n_blocks) * MOE_BLOCK, side='right'), N_EXPERTS - 1)
    xt_pad = jnp.concatenate([xt, jnp.zeros((1, D), xt.dtype)], axis=0)
    x_slots = xt_pad[slot_tok].reshape(n_blocks, MOE_BLOCK, D)

    def expert_block(args):
        xb, e = args
        return (jax.nn.silu(xb @ w1[e]) * (xb @ w3[e])) @ w2[e]

    y = lax.map(expert_block, (x_slots, block_expert)).reshape(P, D)
    out = jnp.zeros((T + 1, D), y.dtype).at[slot_tok].add(y * slot_w[:, None].astype(y.dtype))[:T]
    return out.reshape(B, L, D)


def _nrm(k, shape, scale):
    return jax.random.normal(k, shape, jnp.float32) * scale


def setup_inputs(seed: int = 0) -> dict:
    key = jax.random.key(seed)
    ks = jax.random.split(key, 38)
    D = D_MODEL
    return {
        'x': _nrm(ks[0], (BATCH, SEQ, D), 1.0),
        'mem': _nrm(ks[1], (BATCH, MEM_LEN, D), 1.0),
        'w_in': _nrm(ks[2], (DEPTH, D, IN_COLS), D ** -0.5),
        'sg_w': _nrm(ks[3], (DEPTH, SG_GROUPS, SG_CHUNK, SG_CHUNK), SG_CHUNK ** -0.5),
        'sg_b': 1.0 + _nrm(ks[4], (DEPTH, SG_GROUPS, SG_CHUNK), 0.1),
        'sg_ln_g': 1.0 + _nrm(ks[5], (DEPTH, BRANCH_W), 0.02),
        'sg_ln_b': _nrm(ks[6], (DEPTH, BRANCH_W), 0.02),
        'hg_lb_logits': _nrm(ks[7], (DEPTH, BRANCH_W), 0.5),
        'hg_norm_g': 1.0 + _nrm(ks[8], (DEPTH, BRANCH_W), 0.02),
        'rw_mu': jax.random.uniform(ks[9], (DEPTH, RW_COLS), jnp.float32, 0.0, 1.0),
        'rw_w0': jax.random.uniform(ks[10], (DEPTH, BRANCH_W), jnp.float32, -6.0, 1.0),
        'rw_w2': _nrm(ks[11], (DEPTH, RW_W_LORA, BRANCH_W), 0.1 * RW_W_LORA ** -0.5),
        'rw_a0': _nrm(ks[12], (DEPTH, BRANCH_W), 0.5),
        'rw_a2': _nrm(ks[13], (DEPTH, RW_A_LORA, BRANCH_W), 0.1 * RW_A_LORA ** -0.5),
        'rw_g2': _nrm(ks[14], (DEPTH, RW_G_LORA, BRANCH_W), RW_G_LORA ** -0.5),
        'rw_k_k': 0.85 + _nrm(ks[15], (DEPTH, BRANCH_W), 0.05),
        'rw_k_a': 1.0 + _nrm(ks[16], (DEPTH, BRANCH_W), 0.05),
        'rw_r_k': _nrm(ks[17], (DEPTH, BRANCH_W), 0.1),
        'rw_gn_g': 1.0 + _nrm(ks[18], (DEPTH, BRANCH_W), 0.02),
        'rw_gn_b': _nrm(ks[19], (DEPTH, BRANCH_W), 0.02),
        'rw_v0': 1.0 + _nrm(ks[20], (DEPTH - 1, BRANCH_W), 0.1),
        'rw_v1': _nrm(ks[21], (DEPTH - 1, BRANCH_W, RW_V_LORA), BRANCH_W ** -0.5),
        'rw_v2': _nrm(ks[22], (DEPTH - 1, RW_V_LORA, BRANCH_W), 0.1 * RW_V_LORA ** -0.5),
        'w_branch': _nrm(ks[23], (DEPTH, N_BRANCHES, BRANCH_W, D), BRANCH_W ** -0.5),
        'w_mix_out': _nrm(ks[24], (DEPTH, D, D), DN_BETA * D ** -0.5),
        'xa_wq': _nrm(ks[25], (DEPTH, D, XA_W), D ** -0.5),
        'xa_wk': _nrm(ks[26], (DEPTH, D, XA_W), D ** -0.5),
        'xa_wv': _nrm(ks[27], (DEPTH, D, XA_W), DN_BETA * D ** -0.5),
        'xa_wo': _nrm(ks[28], (DEPTH, XA_W, D), DN_BETA * XA_W ** -0.5),
        'ln_g': 1.0 + _nrm(ks[29], (DEPTH, 3, D), 0.02),
        'ln_b': _nrm(ks[30], (DEPTH, 3, D), 0.02),
        'ffn_w1': _nrm(ks[31], (N_DENSE, D, D_FF), DN_BETA * D ** -0.5),
        'ffn_w3': _nrm(ks[32], (N_DENSE, D, D_FF), DN_BETA * D ** -0.5),
        'ffn_w2': _nrm(ks[33], (N_DENSE, D_FF, D), DN_BETA * D_FF ** -0.5),
        'moe_router': _nrm(ks[34], (N_MOE, D, N_EXPERTS), D ** -0.5),
        'moe_w1': _nrm(ks[35], (N_MOE, N_EXPERTS, D, D_FF), DN_BETA * D ** -0.5),
        'moe_w3': _nrm(ks[36], (N_MOE, N_EXPERTS, D, D_FF), DN_BETA * D ** -0.5),
        'moe_w2': _nrm(ks[37], (N_MOE, N_EXPERTS, D_FF, D), DN_BETA * D_FF ** -0.5),
    }


def reference(x, mem, w_in, sg_w, sg_b, sg_ln_g, sg_ln_b, hg_lb_logits, hg_norm_g,
              rw_mu, rw_w0, rw_w2, rw_a0, rw_a2, rw_g2, rw_k_k, rw_k_a, rw_r_k, rw_gn_g, rw_gn_b,
              rw_v0, rw_v1, rw_v2, w_branch, w_mix_out, xa_wq, xa_wk, xa_wv, xa_wo, ln_g, ln_b,
              ffn_w1, ffn_w3, ffn_w2, moe_router, moe_w1, moe_w3, moe_w2):
    p = jax.nn.softmax(hg_lb_logits.astype(jnp.float32), axis=0)
    lower_bounds = jnp.cumsum(p, axis=0) - p[0]
    B, L, _ = x.shape
    v_first = None
    for l in range(DEPTH):
        proj = x @ w_in[l]
        z_sg, z_hg, z_rw, z_gate = jnp.split(proj, IN_SPLITS, axis=-1)
        y_sg = spatial_gating_branch(z_sg, sg_w[l], sg_b[l], sg_ln_g[l], sg_ln_b[l])
        zq, zf, zi, zg = jnp.split(z_hg, 4, axis=-1)
        y_hg = hgrn2_branch(zq, zf, zi, zg, lower_bounds[l], hg_norm_g[l])
        v_mix = None if l == 0 else (rw_v0[l - 1], rw_v1[l - 1], rw_v2[l - 1])
        y_rw, v_first = rwkv7_branch(z_rw, rw_mu[l], rw_w0[l], rw_w2[l], rw_a0[l], rw_a2[l], rw_g2[l],
                                     rw_k_k[l], rw_k_a[l], rw_r_k[l], rw_gn_g[l], rw_gn_b[l],
                                     v_first, v_mix)
        gates = jax.nn.sigmoid(z_gate).reshape(B, L, N_BRANCHES, D_MODEL)
        merged = (gates[:, :, 0] * (y_sg.astype(x.dtype) @ w_branch[l, 0])
                  + gates[:, :, 1] * (y_hg.astype(x.dtype) @ w_branch[l, 1])
                  + gates[:, :, 2] * (y_rw.astype(x.dtype) @ w_branch[l, 2]))
        x = layer_norm(DN_ALPHA * x + merged @ w_mix_out[l], ln_g[l, 0], ln_b[l, 0])
        xa = cross_attention(x, mem, xa_wq[l], xa_wk[l], xa_wv[l], xa_wo[l])
        x = layer_norm(DN_ALPHA * x + xa, ln_g[l, 1], ln_b[l, 1])
        if l % 2 == 0:
            h = swiglu(x, ffn_w1[l // 2], ffn_w3[l // 2], ffn_w2[l // 2])
        else:
            h = moe_swiglu(x, moe_router[l // 2], moe_w1[l // 2], moe_w3[l // 2], moe_w2[l // 2])
        x = layer_norm(DN_ALPHA * x + h, ln_g[l, 2], ln_b[l, 2])
    return x
```

```python
import functools
import math

import jax
import jax.numpy as jnp
from jax import lax
from jax.experimental import pallas as pl
from jax.experimental.pallas import tpu as pltpu

F32 = jnp.float32
BF16 = jnp.bfloat16

V7X_VMEM_BYTES = 64 * 1024 * 1024
VMEM_LIMIT = V7X_VMEM_BYTES - 8 * 1024 * 1024
LANES = 128

BRANCH_W = 512
SG_CHUNK = 128
SG_GROUPS = 4
HG_HEADS = 4
HG_DK = BRANCH_W // HG_HEADS
HG_CHUNK = 16
HG_EPS = 1e-6
HG_EXP_CLIP = 60.0
RW_HEAD = 64
RW_HEADS = BRANCH_W // RW_HEAD
RW_CHUNK = 64
RW_W_LORA = 96
RW_A_LORA = 96
RW_G_LORA = 256
RW_GN_EPS = 64e-5
RW_LORA_PAD = 128
XA_HEADS = 4
XA_HEAD_DIM = 128
N_EXPERTS = 8
MOE_ROWS = 512
LN_EPS = 1e-5


def _params(*sem):
    return pltpu.CompilerParams(dimension_semantics=sem, vmem_limit_bytes=VMEM_LIMIT)


def _dot(a, b):
    return jnp.dot(a.astype(BF16), b.astype(BF16), preferred_element_type=F32)


def _dot_nt(a, b):
    return lax.dot_general(a.astype(BF16), b.astype(BF16), (((1,), (1,)), ((), ())),
                           preferred_element_type=F32)


def _dot_tn(a, b):
    return lax.dot_general(a.astype(BF16), b.astype(BF16), (((0,), (0,)), ((), ())),
                           preferred_element_type=F32)


def _split3(x):
    hi = x.astype(BF16)
    r1 = x - hi.astype(F32)
    mid = r1.astype(BF16)
    lo = (r1 - mid.astype(F32)).astype(BF16)
    return hi, mid, lo


def _dot_exact_lhs(m_bf16, x):
    hi, mid, lo = _split3(x)
    return (jnp.dot(m_bf16, hi, preferred_element_type=F32)
            + jnp.dot(m_bf16, mid, preferred_element_type=F32)
            + jnp.dot(m_bf16, lo, preferred_element_type=F32))


def _dot_exact_rhs(x, m_bf16):
    hi, mid, lo = _split3(x)
    return (jnp.dot(hi, m_bf16, preferred_element_type=F32)
            + jnp.dot(mid, m_bf16, preferred_element_type=F32)
            + jnp.dot(lo, m_bf16, preferred_element_type=F32))


def _head_sum(x, ones_blk):
    n = x.shape[-1] // LANES
    return jnp.concatenate(
        [_dot_exact_rhs(x[:, i * LANES:(i + 1) * LANES], ones_blk) for i in range(n)], axis=-1)


def _sigmoid(x):
    return 1.0 / (1.0 + jnp.exp(-x))


def _log_sigmoid(x):
    return jnp.minimum(x, 0.0) - jnp.log1p(jnp.exp(-jnp.abs(x)))


def _layer_norm(x, g, b, eps):
    mu = jnp.mean(x, axis=-1, keepdims=True)
    xc = x - mu
    var = jnp.mean(xc * xc, axis=-1, keepdims=True)
    return xc * lax.rsqrt(var + eps) * g + b


def _tril_mask(n, strict=False):
    row = lax.broadcasted_iota(jnp.int32, (n, n), 0)
    col = lax.broadcasted_iota(jnp.int32, (n, n), 1)
    return (row > col) if strict else (row >= col)


def _mm_kernel(a_ref, b_ref, o_ref):
    o_ref[...] = jnp.dot(a_ref[...], b_ref[...], preferred_element_type=F32).astype(o_ref.dtype)


def matmul(a, b, out_dtype, tm=1024, tn=512):
    m, k = a.shape
    n = b.shape[1]
    tm, tn = min(tm, m), min(tn, n)
    return pl.pallas_call(
        _mm_kernel,
        grid=(m // tm, n // tn),
        in_specs=[pl.BlockSpec((tm, k), lambda i, j: (i, 0)),
                  pl.BlockSpec((k, tn), lambda i, j: (0, j))],
        out_specs=pl.BlockSpec((tm, tn), lambda i, j: (i, j)),
        out_shape=jax.ShapeDtypeStruct((m, n), out_dtype),
        compiler_params=_params("parallel", "parallel"),
    )(a, b)


def _mm_res_ln_kernel(a_ref, w_ref, res_ref, g_ref, b_ref, of_ref, ob_ref, *, alpha):
    h = jnp.dot(a_ref[...], w_ref[...], preferred_element_type=F32)
    y = _layer_norm(alpha * res_ref[...] + h, g_ref[...], b_ref[...], LN_EPS)
    of_ref[...] = y
    ob_ref[...] = y.astype(BF16)


def matmul_res_ln(a, w, res, g, b, alpha, tm=512):
    m, k = a.shape
    n = w.shape[1]
    tm = min(tm, m)
    return pl.pallas_call(
        functools.partial(_mm_res_ln_kernel, alpha=alpha),
        grid=(m // tm,),
        in_specs=[pl.BlockSpec((tm, k), lambda i: (i, 0)),
                  pl.BlockSpec((k, n), lambda i: (0, 0)),
                  pl.BlockSpec((tm, n), lambda i: (i, 0)),
                  pl.BlockSpec((1, n), lambda i: (0, 0)),
                  pl.BlockSpec((1, n), lambda i: (0, 0))],
        out_specs=[pl.BlockSpec((tm, n), lambda i: (i, 0)),
                   pl.BlockSpec((tm, n), lambda i: (i, 0))],
        out_shape=[jax.ShapeDtypeStruct((m, n), F32), jax.ShapeDtypeStruct((m, n), BF16)],
        compiler_params=_params("parallel"),
    )(a, w, res, g.reshape(1, n), b.reshape(1, n))


def _sg_kernel(z_ref, w_ref, b_ref, g_ref, beta_ref, y_ref, *, n_chunks):
    z = z_ref[...]
    z = 0.5 * z * (1.0 + jnp.tanh(math.sqrt(2.0 / math.pi) * (z + 0.044715 * (z * z * z))))
    u = z[:, :BRANCH_W]
    v = _layer_norm(z[:, BRANCH_W:], g_ref[...], beta_ref[...], LN_EPS).astype(BF16)
    causal = _tril_mask(SG_CHUNK)
    gw = BRANCH_W // SG_GROUPS
    for g in range(SG_GROUPS):
        wg = jnp.where(causal, w_ref[g], 0.0).astype(BF16)
        for c in range(n_chunks):
            rows = slice(c * SG_CHUNK, (c + 1) * SG_CHUNK)
            cols = slice(g * gw, (g + 1) * gw)
            s = jnp.dot(wg, v[rows, cols], preferred_element_type=F32) + b_ref[g]
            y_ref[rows, cols] = (u[rows, cols] * s).astype(BF16)


def sg_branch(zall, col_block, sg_w, sg_b, ln_g, ln_b, tm=512):
    t = zall.shape[0]
    tm = min(tm, t)
    return pl.pallas_call(
        functools.partial(_sg_kernel, n_chunks=tm // SG_CHUNK),
        grid=(t // tm,),
        in_specs=[pl.BlockSpec((tm, 2 * BRANCH_W), lambda i: (i, col_block)),
                  pl.BlockSpec((SG_GROUPS, SG_CHUNK, SG_CHUNK), lambda i: (0, 0, 0)),
                  pl.BlockSpec((SG_GROUPS, SG_CHUNK, 1), lambda i: (0, 0, 0)),
                  pl.BlockSpec((1, BRANCH_W), lambda i: (0, 0)),
                  pl.BlockSpec((1, BRANCH_W), lambda i: (0, 0))],
        out_specs=pl.BlockSpec((tm, BRANCH_W), lambda i: (i, 0)),
        out_shape=jax.ShapeDtypeStruct((t, BRANCH_W), BF16),
        compiler_params=_params("parallel"),
    )(zall, sg_w, sg_b.reshape(SG_GROUPS, SG_CHUNK, 1), ln_g.reshape(1, -1), ln_b.reshape(1, -1))


def _hgrn_kernel(z_ref, lb_ref, ng_ref, y_ref, st_ref, *, batch):
    c = HG_CHUNK

    @pl.when(pl.program_id(0) == 0)
    def _():
        st_ref[...] = jnp.zeros_like(st_ref)

    lb = lb_ref[...]
    ltri = jnp.where(_tril_mask(c), 1.0, 0.0).astype(BF16)
    rows = lax.broadcasted_iota(jnp.int32, (c, 1), 0)
    w = BRANCH_W
    for b in range(batch):
        zq = z_ref[b, :, 0:w]
        zf = z_ref[b, :, w:2 * w]
        zi = z_ref[b, :, 2 * w:3 * w]
        zg = z_ref[b, :, 3 * w:4 * w]
        log_f = _log_sigmoid(zf) + jnp.log1p(lb * jnp.exp(jnp.minimum(-zf, HG_EXP_CLIP)))
        log_f = jnp.minimum(log_f, 0.0)
        kx = (1.0 - lb) * _sigmoid(-zf)
        q = zq * _sigmoid(zq)
        bc = _dot_exact_lhs(ltri, log_f)
        b_last = bc[c - 1:c, :]
        qd = q * jnp.exp(bc)
        kd = kx * jnp.exp(b_last - bc)
        e_last = jnp.exp(b_last)
        outs = []
        for h in range(HG_HEADS):
            sl = slice(h * HG_DK, (h + 1) * HG_DK)
            st = st_ref[b * HG_HEADS + h]
            o = _dot_nt(qd[:, sl], st)
            qh, kh, bh, vh = q[:, sl], kx[:, sl], bc[:, sl], zi[:, sl]
            for s in range(c):
                m = qh * kh[s:s + 1, :] * jnp.exp(jnp.minimum(bh - bh[s:s + 1, :], 0.0))
                att = jnp.where(rows >= s, jnp.sum(m, axis=-1, keepdims=True), 0.0)
                o = o + att * vh[s:s + 1, :]
            st_ref[b * HG_HEADS + h] = st * e_last[:, sl] + _dot_tn(vh, kd[:, sl])
            outs.append(o)
        o = jnp.concatenate(outs, axis=-1) * _sigmoid(zg)
        y = o * lax.rsqrt(jnp.mean(o * o, axis=-1, keepdims=True) + HG_EPS) * ng_ref[...]
        y_ref[b] = y.astype(BF16)


def hgrn2_branch(z3, col_block, lb, norm_g):
    batch, seq, _ = z3.shape
    return pl.pallas_call(
        functools.partial(_hgrn_kernel, batch=batch),
        grid=(seq // HG_CHUNK,),
        in_specs=[pl.BlockSpec((batch, HG_CHUNK, 4 * BRANCH_W), lambda c: (0, c, col_block)),
                  pl.BlockSpec((1, BRANCH_W), lambda c: (0, 0)),
                  pl.BlockSpec((1, BRANCH_W), lambda c: (0, 0))],
        out_specs=pl.BlockSpec((batch, HG_CHUNK, BRANCH_W), lambda c: (0, c, 0)),
        out_shape=jax.ShapeDtypeStruct((batch, seq, BRANCH_W), BF16),
        scratch_shapes=[pltpu.VMEM((batch * HG_HEADS, HG_DK, HG_DK), F32)],
        compiler_params=_params("arbitrary"),
    )(z3, lb.reshape(1, -1), norm_g.reshape(1, -1))


def _rw_prep_kernel(*refs, tiles_per_seq, has_vmix):
    if has_vmix:
        (z_ref, zp_ref, mu_ref, w0_ref, w2_ref, a0_ref, a2_ref, g2_ref, kk_ref, ka_ref, ones_ref,
         v0_ref, v1_ref, v2_ref, vf_ref, r_o, lw_o, k_o, v_o, kk_o, al_o, g_o) = refs
    else:
        (z_ref, zp_ref, mu_ref, w0_ref, w2_ref, a0_ref, a2_ref, g2_ref, kk_ref, ka_ref, ones_ref,
         r_o, lw_o, k_o, v_o, kk_o, al_o, g_o) = refs
    z = z_ref[...]
    tm = z.shape[0]
    first = (pl.program_id(0) % tiles_per_seq) == 0
    prev = jnp.where(first, 0.0, zp_ref[7:8, :])
    rows = lax.broadcasted_iota(jnp.int32, (tm, 1), 0)
    zs = jnp.where(rows == 0, prev, pltpu.roll(z, 1, 0))
    zc = z + (zs - z) * mu_ref[...]
    w = BRANCH_W
    p = RW_LORA_PAD
    r, k, v = zc[:, 0:w], zc[:, w:2 * w], zc[:, 2 * w:3 * w]
    wl, al_in, gl = zc[:, 3 * w:3 * w + p], zc[:, 3 * w + p:3 * w + 2 * p], zc[:, 3 * w + 2 * p:]
    u = w0_ref[...] + _dot(jnp.tanh(wl), w2_ref[...])
    lw = -math.exp(-0.5) * _sigmoid(u)
    a = _sigmoid(a0_ref[...] + _dot(al_in, a2_ref[...]))
    g = _dot(_sigmoid(gl), g2_ref[...])
    if has_vmix:
        mix = _sigmoid(v0_ref[...] + _dot(_dot(v, v1_ref[...]), v2_ref[...]))
        v = v + (vf_ref[...] - v) * mix
    kkx = k * kk_ref[...]
    norm = jnp.sqrt(_head_sum(kkx * kkx, ones_ref[...]))
    kkn = kkx / jnp.maximum(norm, 1e-12)
    r_o[...] = r
    lw_o[...] = lw
    k_o[...] = k * (1.0 + (a - 1.0) * ka_ref[...])
    v_o[...] = v
    kk_o[...] = kkn
    al_o[...] = a
    g_o[...] = g


def rw_prep(zall, col_block, seq, mu, w0, w2p, a0, a2p, g2, k_k, k_a, ones_blk, vmix, v_first, tm=256):
    t = zall.shape[0]
    tm = min(tm, seq)
    zw = 4 * BRANCH_W
    row = lambda i: (i, 0)
    const = lambda i: (0, 0)
    vec = pl.BlockSpec((1, BRANCH_W), const)
    in_specs = [pl.BlockSpec((tm, zw), lambda i: (i, col_block)),
                pl.BlockSpec((8, zw), lambda i: (jnp.maximum(i * (tm // 8) - 1, 0), col_block)),
                pl.BlockSpec((1, zw), const), vec,
                pl.BlockSpec((RW_LORA_PAD, BRANCH_W), const), vec,
                pl.BlockSpec((RW_LORA_PAD, BRANCH_W), const),
                pl.BlockSpec((RW_G_LORA, BRANCH_W), const), vec, vec,
                pl.BlockSpec((LANES, LANES), const)]
    args = [zall, zall, mu.reshape(1, -1), w0.reshape(1, -1), w2p, a0.reshape(1, -1), a2p, g2,
            k_k.reshape(1, -1), k_a.reshape(1, -1), ones_blk]
    if vmix is not None:
        v0, v1, v2 = vmix
        in_specs += [vec, pl.BlockSpec(v1.shape, const), pl.BlockSpec(v2.shape, const),
                     pl.BlockSpec((tm, BRANCH_W), row)]
        args += [v0.reshape(1, -1), v1, v2, v_first]
    out = jax.ShapeDtypeStruct((t, BRANCH_W), F32)
    return pl.pallas_call(
        functools.partial(_rw_prep_kernel, tiles_per_seq=seq // tm, has_vmix=vmix is not None),
        grid=(t // tm,),
        in_specs=in_specs,
        out_specs=[pl.BlockSpec((tm, BRANCH_W), row)] * 7,
        out_shape=[out] * 7,
        compiler_params=_params("parallel"),
    )(*args)


def _rwkv_kernel(r_ref, lw_ref, k_ref, v_ref, kk_ref, al_ref, g_ref, gng_ref, gnb_ref, rk_ref, ones_ref,
                 y_ref, st_ref):
    c = RW_CHUNK

    @pl.when(pl.program_id(1) == 0)
    def _():
        st_ref[...] = jnp.zeros_like(st_ref)

    incl = _tril_mask(c)
    strict = _tril_mask(c, strict=True)
    ltri = jnp.where(incl, 1.0, 0.0).astype(BF16)
    r, lw, k, v, kk, al = r_ref[0], lw_ref[0], k_ref[0], v_ref[0], kk_ref[0], al_ref[0]
    cum = _dot_exact_lhs(ltri, lw)
    gam = jnp.exp(cum)
    ginv = jnp.exp(-cum)
    at = -kk * jnp.exp(cum - lw)
    rt = r * gam
    bt = kk * al * ginv
    kt = k * ginv
    g_last = gam[c - 1:c, :]
    ys = []
    for h in range(RW_HEADS):
        sl = slice(h * RW_HEAD, (h + 1) * RW_HEAD)
        a_h, r_h, b_h, k_h, v_h = at[:, sl], rt[:, sl], bt[:, sl], kt[:, sl], v[:, sl]
        st = st_ref[h]
        n = jnp.where(strict, _dot_nt(a_h, b_h), 0.0)
        ak = jnp.where(strict, _dot_nt(a_h, k_h), 0.0)
        rb = jnp.where(incl, _dot_nt(r_h, b_h), 0.0)
        rk = jnp.where(incl, _dot_nt(r_h, k_h), 0.0)
        x = _dot_nt(a_h, st) + _dot(ak, v_h)
        u = x + _dot(n, x)
        p = n
        for _ in range(int(math.log2(c)) - 1):
            p = _dot(p, p)
            u = u + _dot(p, u)
        ys.append(_dot_nt(r_h, st) + _dot(rb, u) + _dot(rk, v_h))
        st_ref[h] = (st + _dot_tn(u, b_h) + _dot_tn(v_h, k_h)) * g_last[:, sl]
    y = jnp.concatenate(ys, axis=-1)
    ones = ones_ref[...]
    inv_n = 1.0 / RW_HEAD
    mu_y = _head_sum(y, ones) * inv_n
    yc = y - mu_y
    var = _head_sum(yc * yc, ones) * inv_n
    yn = yc * lax.rsqrt(var + RW_GN_EPS) * gng_ref[...] + gnb_ref[...]
    bonus = _head_sum(r * k * rk_ref[...], ones)
    y_ref[0] = ((yn + bonus * v) * g_ref[0]).astype(BF16)


def rwkv7_mix(r, lw, k, v, kk, al, g, gn_g, gn_b, r_k, ones_blk, batch, seq):
    arrs = [x.reshape(batch, seq, BRANCH_W) for x in (r, lw, k, v, kk, al, g)]
    blk = pl.BlockSpec((1, RW_CHUNK, BRANCH_W), lambda b, c: (b, c, 0))
    vec = pl.BlockSpec((1, BRANCH_W), lambda b, c: (0, 0))
    return pl.pallas_call(
        _rwkv_kernel,
        grid=(batch, seq // RW_CHUNK),
        in_specs=[blk] * 7 + [vec] * 3 + [pl.BlockSpec((LANES, LANES), lambda b, c: (0, 0))],
        out_specs=blk,
        out_shape=jax.ShapeDtypeStruct((batch, seq, BRANCH_W), BF16),
        scratch_shapes=[pltpu.VMEM((RW_HEADS, RW_HEAD, RW_HEAD), F32)],
        compiler_params=_params("parallel", "arbitrary"),
    )(*arrs, gn_g.reshape(1, -1), gn_b.reshape(1, -1), r_k.reshape(1, -1), ones_blk)


def _merge_kernel(x_ref, ysg_ref, yhg_ref, yrw_ref, wg_ref, wb_ref, o_ref):
    x = x_ref[...]
    acc = None
    for b, y_ref in enumerate((ysg_ref, yhg_ref, yrw_ref)):
        gate = _sigmoid(jnp.dot(x, wg_ref[b], preferred_element_type=F32))
        term = gate * jnp.dot(y_ref[...], wb_ref[b], preferred_element_type=F32)
        acc = term if acc is None else acc + term
    o_ref[...] = acc.astype(BF16)


def merge_branches(xb, y_sg, y_hg, y_rw, w_gate, w_branch, tm=1024, tn=512):
    t, d = xb.shape
    tm = min(tm, t)
    ysp = pl.BlockSpec((tm, BRANCH_W), lambda i, j: (i, 0))
    return pl.pallas_call(
        _merge_kernel,
        grid=(t // tm, d // tn),
        in_specs=[pl.BlockSpec((tm, d), lambda i, j: (i, 0)), ysp, ysp, ysp,
                  pl.BlockSpec((3, d, tn), lambda i, j: (0, 0, j)),
                  pl.BlockSpec((3, BRANCH_W, tn), lambda i, j: (0, 0, j))],
        out_specs=pl.BlockSpec((tm, tn), lambda i, j: (i, j)),
        out_shape=jax.ShapeDtypeStruct((t, d), BF16),
        compiler_params=_params("parallel", "parallel"),
    )(xb, y_sg, y_hg, y_rw, w_gate, w_branch)


def _xattn_kernel(xb_ref, xf_ref, k_ref, v_ref, wq_ref, wo_ref, g_ref, b_ref, of_ref, ob_ref, *, alpha):
    q = jnp.dot(xb_ref[0], wq_ref[...], preferred_element_type=F32)
    kmem, vmem = k_ref[0], v_ref[0]
    scale = XA_HEAD_DIM ** -0.5
    outs = []
    for h in range(XA_HEADS):
        sl = slice(h * XA_HEAD_DIM, (h + 1) * XA_HEAD_DIM)
        s = _dot_nt(q[:, sl], kmem[:, sl]) * scale
        e = jnp.exp(s - jnp.max(s, axis=-1, keepdims=True))
        p = e / jnp.sum(e, axis=-1, keepdims=True)
        outs.append(_dot(p, vmem[:, sl]))
    o = jnp.concatenate(outs, axis=-1)
    h_out = _dot(o, wo_ref[...])
    y = _layer_norm(alpha * xf_ref[0] + h_out, g_ref[...], b_ref[...], LN_EPS)
    of_ref[0] = y
    ob_ref[0] = y.astype(BF16)


def cross_attention_ln(xb, xf, kmem, vmem, wq, wo, g, b, alpha, tm=512):
    batch, seq, d = xf.shape
    m, xw = kmem.shape[1], kmem.shape[2]
    tm = min(tm, seq)
    xs = pl.BlockSpec((1, tm, d), lambda bi, i: (bi, i, 0))
    ms = pl.BlockSpec((1, m, xw), lambda bi, i: (bi, 0, 0))
    vec = pl.BlockSpec((1, d), lambda bi, i: (0, 0))
    return pl.pallas_call(
        functools.partial(_xattn_kernel, alpha=alpha),
        grid=(batch, seq // tm),
        in_specs=[xs, xs, ms, ms,
                  pl.BlockSpec((d, xw), lambda bi, i: (0, 0)),
                  pl.BlockSpec((xw, d), lambda bi, i: (0, 0)), vec, vec],
        out_specs=[xs, xs],
        out_shape=[jax.ShapeDtypeStruct((batch, seq, d), F32), jax.ShapeDtypeStruct((batch, seq, d), BF16)],
        compiler_params=_params("parallel", "parallel"),
    )(xb, xf, kmem, vmem, wq, wo, g.reshape(1, d), b.reshape(1, d))


def _ffn_kernel(be_ref, x_ref, w1_ref, w3_ref, w2_ref, g_ref, b_ref, *rest, alpha, final_ln):
    del be_ref
    if final_ln:
        of_ref, ob_ref, xb_scr, acc_scr = rest
    else:
        of_ref, xb_scr, acc_scr = rest
    f = pl.program_id(1)

    @pl.when(f == 0)
    def _():
        xb_scr[...] = x_ref[...].astype(BF16)

    xb = xb_scr[...]
    gate = jnp.dot(xb, w1_ref[0], preferred_element_type=F32)
    up = jnp.dot(xb, w3_ref[0], preferred_element_type=F32)
    hh = (gate * _sigmoid(gate) * up).astype(BF16)
    contrib = jnp.dot(hh, w2_ref[0], preferred_element_type=F32)

    @pl.when(f == 0)
    def _():
        acc_scr[...] = contrib

    @pl.when(f > 0)
    def _():
        acc_scr[...] += contrib

    @pl.when(f == pl.num_programs(1) - 1)
    def _():
        if final_ln:
            y = _layer_norm(alpha * x_ref[...] + acc_scr[...], g_ref[...], b_ref[...], LN_EPS)
            of_ref[...] = y
            ob_ref[...] = y.astype(BF16)
        else:
            of_ref[...] = acc_scr[...]


def swiglu(x, block_expert, w1, w3, w2, g, b, alpha, final_ln, tm=512, tf=512):
    rows, d = x.shape
    ff = w1.shape[2]
    tm = min(tm, rows)
    xs = pl.BlockSpec((tm, d), lambda i, f, be: (i, 0))
    vec = pl.BlockSpec((1, d), lambda i, f, be: (0, 0))
    n_out = 2 if final_ln else 1
    out_shape = [jax.ShapeDtypeStruct((rows, d), F32), jax.ShapeDtypeStruct((rows, d), BF16)][:n_out]
    res = pl.pallas_call(
        functools.partial(_ffn_kernel, alpha=alpha, final_ln=final_ln),
        grid_spec=pltpu.PrefetchScalarGridSpec(
            num_scalar_prefetch=1,
            grid=(rows // tm, ff // tf),
            in_specs=[xs,
                      pl.BlockSpec((1, d, tf), lambda i, f, be: (be[i], 0, f)),
                      pl.BlockSpec((1, d, tf), lambda i, f, be: (be[i], 0, f)),
                      pl.BlockSpec((1, tf, d), lambda i, f, be: (be[i], f, 0)),
                      vec, vec],
            out_specs=[xs] * n_out,
            scratch_shapes=[pltpu.VMEM((tm, d), BF16), pltpu.VMEM((tm, d), F32)]),
        out_shape=out_shape,
        compiler_params=_params("parallel", "arbitrary"),
    )(block_expert, x, w1, w3, w2, g.reshape(1, d), b.reshape(1, d))
    return res if final_ln else res[0]


def _router_kernel(x_ref, w_ref, o_ref):
    x = x_ref[...]
    hi, mid, lo = _split3(x)
    whi, wmid, wlo = _split3(w_ref[...])
    dot = lambda a, b: jnp.dot(a, b, preferred_element_type=F32)
    logits = (dot(hi, whi) + (dot(hi, wmid) + dot(mid, whi))
              + (dot(mid, wmid) + dot(hi, wlo) + dot(lo, whi)))
    lane = lax.broadcasted_iota(jnp.int32, logits.shape, 1)
    neg = -jnp.inf
    lg = jnp.where(lane < N_EXPERTS, logits, neg)
    m1 = jnp.max(lg, axis=-1, keepdims=True)
    i1 = jnp.min(jnp.where(lg == m1, lane, LANES), axis=-1, keepdims=True)
    lg2 = jnp.where(lane == i1, neg, lg)
    m2 = jnp.max(lg2, axis=-1, keepdims=True)
    i2 = jnp.min(jnp.where(lg2 == m2, lane, LANES), axis=-1, keepdims=True)
    e = jnp.exp(m2 - m1)
    g1 = 1.0 / (1.0 + e)
    g2 = e / (1.0 + e)
    out = jnp.where(lane == 0, i1.astype(F32),
                    jnp.where(lane == 1, i2.astype(F32),
                              jnp.where(lane == 2, g1, jnp.where(lane == 3, g2, 0.0))))
    o_ref[...] = out


def route_top2(x, w_router_pad, tm=1024):
    t, d = x.shape
    tm = min(tm, t)
    return pl.pallas_call(
        _router_kernel,
        grid=(t // tm,),
        in_specs=[pl.BlockSpec((tm, d), lambda i: (i, 0)), pl.BlockSpec((d, LANES), lambda i: (0, 0))],
        out_specs=pl.BlockSpec((tm, LANES), lambda i: (i, 0)),
        out_shape=jax.ShapeDtypeStruct((t, LANES), F32),
        compiler_params=_params("parallel"),
    )(x, w_router_pad)


def _row_copy(src_hbm, dst_ref, sem, src_row, dst_row):
    return pltpu.make_async_copy(src_hbm.at[pl.ds(src_row, 1)], dst_ref.at[pl.ds(dst_row, 1)], sem)


def _gather_rows_kernel(idx_ref, src_hbm, dst_hbm, sem, *, rows):
    base = pl.program_id(0) * rows

    def start(r, carry):
        _row_copy(src_hbm, dst_hbm, sem, idx_ref[0, 0, r], base + r).start()
        return carry

    def wait(r, carry):
        _row_copy(src_hbm, dst_hbm, sem, 0, base + r).wait()
        return carry

    lax.fori_loop(0, rows, start, 0)
    lax.fori_loop(0, rows, wait, 0)


def gather_rows(src, idx, rows=512):
    n = idx.shape[0]
    d = src.shape[1]
    return pl.pallas_call(
        functools.partial(_gather_rows_kernel, rows=rows),
        grid=(n // rows,),
        in_specs=[pl.BlockSpec((1, 1, rows), lambda i: (i, 0, 0), memory_space=pltpu.SMEM),
                  pl.BlockSpec(memory_space=pl.ANY)],
        out_specs=pl.BlockSpec(memory_space=pl.ANY),
        out_shape=jax.ShapeDtypeStruct((n, d), src.dtype),
        scratch_shapes=[pltpu.SemaphoreType.DMA(())],
        compiler_params=_params("arbitrary"),
    )(idx.reshape(n // rows, 1, rows), src)


def _combine_kernel(idx_ref, y_hbm, x_ref, gate_ref, g_ref, b_ref, of_ref, ob_ref, ybuf, sem, *, alpha):
    tm = x_ref.shape[0]

    def start(r, carry):
        _row_copy(y_hbm, ybuf, sem, idx_ref[0, 0, r], r).start()
        return carry

    def wait(r, carry):
        _row_copy(y_hbm, ybuf, sem, 0, r).wait()
        return carry

    lax.fori_loop(0, 2 * tm, start, 0)
    lax.fori_loop(0, 2 * tm, wait, 0)
    gates = gate_ref[...]
    h = gates[:, 0:1] * ybuf[0:tm, :] + gates[:, 1:2] * ybuf[tm:2 * tm, :]
    y = _layer_norm(alpha * x_ref[...] + h, g_ref[...], b_ref[...], LN_EPS)
    of_ref[...] = y
    ob_ref[...] = y.astype(BF16)


def moe_combine_ln(y_slots, dest, gates, x, g, b, alpha, tm=256):
    t, d = x.shape
    tm = min(tm, t)
    idx = dest.reshape(t // tm, tm, 2).transpose(0, 2, 1).reshape(t // tm, 1, 2 * tm)
    row = lambda i: (i, 0)
    vec = pl.BlockSpec((1, d), lambda i: (0, 0))
    return pl.pallas_call(
        functools.partial(_combine_kernel, alpha=alpha),
        grid=(t // tm,),
        in_specs=[pl.BlockSpec((1, 1, 2 * tm), lambda i: (i, 0, 0), memory_space=pltpu.SMEM),
                  pl.BlockSpec(memory_space=pl.ANY),
                  pl.BlockSpec((tm, d), row), pl.BlockSpec((tm, 2), row), vec, vec],
        out_specs=[pl.BlockSpec((tm, d), row), pl.BlockSpec((tm, d), row)],
        out_shape=[jax.ShapeDtypeStruct((t, d), F32), jax.ShapeDtypeStruct((t, d), BF16)],
        scratch_shapes=[pltpu.VMEM((2 * tm, d), F32), pltpu.SemaphoreType.DMA(())],
        compiler_params=_params("arbitrary"),
    )(idx, y_slots, x, gates, g.reshape(1, d), b.reshape(1, d))


def moe_swiglu_ln(xf, w_router_pad, w1, w3, w2, g, b, alpha):
    t, d = xf.shape
    routed = route_top2(xf, w_router_pad)
    expert = routed[:, 0:2].astype(jnp.int32)
    gates = routed[:, 2:4]
    flat_e = expert.reshape(-1)
    onehot = (flat_e[:, None] == jnp.arange(N_EXPERTS, dtype=jnp.int32)[None, :]).astype(jnp.int32)
    rank = jnp.sum((jnp.cumsum(onehot, axis=0) - 1) * onehot, axis=1)
    counts = jnp.sum(onehot, axis=0)
    padded = (counts + MOE_ROWS - 1) // MOE_ROWS * MOE_ROWS
    ends = jnp.cumsum(padded)
    dest = (ends - padded)[flat_e] + rank
    n_blocks = (t * 2) // MOE_ROWS + N_EXPERTS
    n_slots = n_blocks * MOE_ROWS
    flat_tok = jnp.arange(t * 2, dtype=jnp.int32) // 2
    slot_tok = jnp.zeros((n_slots,), jnp.int32).at[dest].set(flat_tok)
    block_expert = jnp.minimum(
        jnp.searchsorted(ends, jnp.arange(n_blocks, dtype=jnp.int32) * MOE_ROWS, side='right'),
        N_EXPERTS - 1).astype(jnp.int32)
    x_slots = gather_rows(xf, slot_tok, rows=MOE_ROWS)
    y_slots = swiglu(x_slots, block_expert, w1, w3, w2, g, b, alpha, final_ln=False, tm=MOE_ROWS)
    return moe_combine_ln(y_slots, dest.reshape(t, 2).astype(jnp.int32), gates, xf, g, b, alpha)


def kernel(x, mem, w_in, sg_w, sg_b, sg_ln_g, sg_ln_b, hg_lb_logits, hg_norm_g, rw_mu, rw_w0, rw_w2, rw_a0, rw_a2, rw_g2, rw_k_k, rw_k_a, rw_r_k, rw_gn_g, rw_gn_b, rw_v0, rw_v1, rw_v2, w_branch, w_mix_out, xa_wq, xa_wk, xa_wv, xa_wo, ln_g, ln_b, ffn_w1, ffn_w3, ffn_w2, moe_router, moe_w1, moe_w3, moe_w2):
    batch, seq, d = x.shape
    depth = w_in.shape[0]
    t = batch * seq
    alpha = (2 * depth) ** 0.25
    w = BRANCH_W
    sg_cols, hg_cols = 2 * w, 4 * w
    rw_cols = 3 * w + RW_W_LORA + RW_A_LORA + RW_G_LORA
    o_hg, o_rw, o_gate = sg_cols, sg_cols + hg_cols, sg_cols + hg_cols + rw_cols

    p = jax.nn.softmax(hg_lb_logits.astype(F32), axis=0)
    lower_bounds = jnp.cumsum(p, axis=0) - p[0]
    lane_head = jnp.arange(LANES) // RW_HEAD
    ones_blk = (lane_head[:, None] == lane_head[None, :]).astype(BF16)
    zero_block = jnp.zeros((1,), jnp.int32)

    def pad_cols(m, n):
        return jnp.pad(m, ((0, 0), (0, n - m.shape[1])))

    def pad_rows(m, n):
        return jnp.pad(m, ((0, n - m.shape[0]), (0, 0)))

    mem_b = mem.reshape(-1, d).astype(BF16)
    xf = x.reshape(t, d)
    xb = xf.astype(BF16)
    v_first = None
    for l in range(depth):
        wl_ = w_in[l]
        c0 = o_rw + 3 * w
        w_rw = jnp.concatenate([wl_[:, o_rw:c0],
                                pad_cols(wl_[:, c0:c0 + RW_W_LORA], RW_LORA_PAD),
                                pad_cols(wl_[:, c0 + RW_W_LORA:c0 + RW_W_LORA + RW_A_LORA], RW_LORA_PAD),
                                wl_[:, c0 + RW_W_LORA + RW_A_LORA:o_gate]], axis=1)
        w_proj = jnp.concatenate([wl_[:, o_hg:o_rw], w_rw, wl_[:, :o_hg]], axis=1).astype(BF16)
        w_gate = wl_[:, o_gate:].reshape(d, 3, d).transpose(1, 0, 2).astype(BF16)
        mu = rw_mu[l]
        mu_p = jnp.concatenate([mu[:3 * w],
                                jnp.pad(mu[3 * w:3 * w + RW_W_LORA], (0, RW_LORA_PAD - RW_W_LORA)),
                                jnp.pad(mu[3 * w + RW_W_LORA:3 * w + RW_W_LORA + RW_A_LORA],
                                        (0, RW_LORA_PAD - RW_A_LORA)),
                                mu[3 * w + RW_W_LORA + RW_A_LORA:]])

        zall = matmul(xb, w_proj, F32)
        y_sg = sg_branch(zall, 4, sg_w[l], sg_b[l], sg_ln_g[l], sg_ln_b[l])
        y_hg = hgrn2_branch(zall.reshape(batch, seq, -1), 0, lower_bounds[l], hg_norm_g[l]).reshape(t, w)
        vmix = None if l == 0 else (rw_v0[l - 1], rw_v1[l - 1].astype(BF16), rw_v2[l - 1].astype(BF16))
        r_, lw_, k_, v_, kk_, al_, g_ = rw_prep(
            zall, 1, seq, mu_p, rw_w0[l], pad_rows(rw_w2[l], RW_LORA_PAD).astype(BF16), rw_a0[l],
            pad_rows(rw_a2[l], RW_LORA_PAD).astype(BF16), rw_g2[l].astype(BF16), rw_k_k[l], rw_k_a[l],
            ones_blk, vmix, v_first)
        if l == 0:
            v_first = v_
        y_rw = rwkv7_mix(r_, lw_, k_, v_, kk_, al_, g_, rw_gn_g[l], rw_gn_b[l], rw_r_k[l], ones_blk,
                         batch, seq).reshape(t, w)
        merged = merge_branches(xb, y_sg, y_hg, y_rw, w_gate, w_branch[l].astype(BF16))
        xf, xb = matmul_res_ln(merged, w_mix_out[l].astype(BF16), xf, ln_g[l, 0], ln_b[l, 0], alpha)

        kmem = matmul(mem_b, xa_wk[l].astype(BF16), BF16).reshape(batch, -1, XA_HEADS * XA_HEAD_DIM)
        vmem = matmul(mem_b, xa_wv[l].astype(BF16), BF16).reshape(batch, -1, XA_HEADS * XA_HEAD_DIM)
        xf3, xb3 = cross_attention_ln(xb.reshape(batch, seq, d), xf.reshape(batch, seq, d), kmem, vmem,
                                      xa_wq[l].astype(BF16), xa_wo[l].astype(BF16), ln_g[l, 1], ln_b[l, 1], alpha)
        xf, xb = xf3.reshape(t, d), xb3.reshape(t, d)

        if l % 2 == 0:
            i = l // 2
            xf, xb = swiglu(xf, zero_block.repeat(t // min(512, t)), ffn_w1[i:i + 1].astype(BF16),
                            ffn_w3[i:i + 1].astype(BF16), ffn_w2[i:i + 1].astype(BF16),
                            ln_g[l, 2], ln_b[l, 2], alpha, final_ln=True)
        else:
            i = l // 2
            xf, xb = moe_swiglu_ln(xf, pad_cols(moe_router[i], LANES), moe_w1[i].astype(BF16),
                                   moe_w3[i].astype(BF16), moe_w2[i].astype(BF16), ln_g[l, 2], ln_b[l, 2], alpha)
    return xf.reshape(batch, seq, d)
```

```python
import functools
import math

import jax
import jax.numpy as jnp
from jax import lax
from jax.experimental import pallas as pl
from jax.experimental.pallas import tpu as pltpu

F32 = jnp.float32
BF16 = jnp.bfloat16

V7X_VMEM_BYTES = 64 * 1024 * 1024
VMEM_LIMIT = V7X_VMEM_BYTES - 8 * 1024 * 1024
LANES = 128

BRANCH_W = 512
SG_CHUNK = 128
SG_GROUPS = 4
HG_HEADS = 4
HG_DK = BRANCH_W // HG_HEADS
HG_CHUNK = 16
HG_EPS = 1e-6
HG_EXP_CLIP = 60.0
RW_HEAD = 64
RW_HEADS = BRANCH_W // RW_HEAD
RW_CHUNK = 64
RW_W_LORA = 96
RW_A_LORA = 96
RW_G_LORA = 256
RW_GN_EPS = 64e-5
RW_LORA_PAD = 128
XA_HEADS = 4
XA_HEAD_DIM = 128
N_EXPERTS = 8
MOE_ROWS = 512
LN_EPS = 1e-5


def _params(*sem):
    return pltpu.CompilerParams(dimension_semantics=sem, vmem_limit_bytes=VMEM_LIMIT)


def _dot(a, b):
    return jnp.dot(a.astype(BF16), b.astype(BF16), preferred_element_type=F32)


def _dot_nt(a, b):
    return lax.dot_general(a.astype(BF16), b.astype(BF16), (((1,), (1,)), ((), ())),
                           preferred_element_type=F32)


def _dot_tn(a, b):
    return lax.dot_general(a.astype(BF16), b.astype(BF16), (((0,), (0,)), ((), ())),
                           preferred_element_type=F32)


def _split3(x):
    hi = x.astype(BF16)
    r1 = x - hi.astype(F32)
    mid = r1.astype(BF16)
    lo = (r1 - mid.astype(F32)).astype(BF16)
    return hi, mid, lo


def _dot_exact_lhs(m_bf16, x):
    hi, mid, lo = _split3(x)
    return (jnp.dot(m_bf16, hi, preferred_element_type=F32)
            + jnp.dot(m_bf16, mid, preferred_element_type=F32)
            + jnp.dot(m_bf16, lo, preferred_element_type=F32))


def _dot_exact_rhs(x, m_bf16):
    hi, mid, lo = _split3(x)
    return (jnp.dot(hi, m_bf16, preferred_element_type=F32)
            + jnp.dot(mid, m_bf16, preferred_element_type=F32)
            + jnp.dot(lo, m_bf16, preferred_element_type=F32))


def _head_sum(x, ones_blk):
    n = x.shape[-1] // LANES
    return jnp.concatenate(
        [_dot_exact_rhs(x[:, i * LANES:(i + 1) * LANES], ones_blk) for i in range(n)], axis=-1)


def _sigmoid(x):
    return 1.0 / (1.0 + jnp.exp(-x))


def _log_sigmoid(x):
    return jnp.minimum(x, 0.0) - jnp.log1p(jnp.exp(-jnp.abs(x)))


def _layer_norm(x, g, b, eps):
    mu = jnp.mean(x, axis=-1, keepdims=True)
    xc = x - mu
    var = jnp.mean(xc * xc, axis=-1, keepdims=True)
    return xc * lax.rsqrt(var + eps) * g + b


def _tril_mask(n, strict=False):
    row = lax.broadcasted_iota(jnp.int32, (n, n), 0)
    col = lax.broadcasted_iota(jnp.int32, (n, n), 1)
    return (row > col) if strict else (row >= col)


def _mm_kernel(a_ref, b_ref, o_ref):
    o_ref[...] = jnp.dot(a_ref[...], b_ref[...], preferred_element_type=F32).astype(o_ref.dtype)


def matmul(a, b, out_dtype, tm=1024, tn=512):
    m, k = a.shape
    n = b.shape[1]
    tm, tn = min(tm, m), min(tn, n)
    return pl.pallas_call(
        _mm_kernel,
        grid=(m // tm, n // tn),
        in_specs=[pl.BlockSpec((tm, k), lambda i, j: (i, 0)),
                  pl.BlockSpec((k, tn), lambda i, j: (0, j))],
        out_specs=pl.BlockSpec((tm, tn), lambda i, j: (i, j)),
        out_shape=jax.ShapeDtypeStruct((m, n), out_dtype),
        compiler_params=_params("parallel", "parallel"),
    )(a, b)


def _mm_res_ln_kernel(a_ref, w_ref, res_ref, g_ref, b_ref, of_ref, ob_ref, *, alpha):
    h = jnp.dot(a_ref[...], w_ref[...], preferred_element_type=F32)
    y = _layer_norm(alpha * res_ref[...] + h, g_ref[...], b_ref[...], LN_EPS)
    of_ref[...] = y
    ob_ref[...] = y.astype(BF16)


def matmul_res_ln(a, w, res, g, b, alpha, tm=512):
    m, k = a.shape
    n = w.shape[1]
    tm = min(tm, m)
    return pl.pallas_call(
        functools.partial(_mm_res_ln_kernel, alpha=alpha),
        grid=(m // tm,),
        in_specs=[pl.BlockSpec((tm, k), lambda i: (i, 0)),
                  pl.BlockSpec((k, n), lambda i: (0, 0)),
                  pl.BlockSpec((tm, n), lambda i: (i, 0)),
                  pl.BlockSpec((1, n), lambda i: (0, 0)),
                  pl.BlockSpec((1, n), lambda i: (0, 0))],
        out_specs=[pl.BlockSpec((tm, n), lambda i: (i, 0)),
                   pl.BlockSpec((tm, n), lambda i: (i, 0))],
        out_shape=[jax.ShapeDtypeStruct((m, n), F32), jax.ShapeDtypeStruct((m, n), BF16)],
        compiler_params=_params("parallel"),
    )(a, w, res, g.reshape(1, n), b.reshape(1, n))


def _sg_kernel(z_ref, w_ref, b_ref, g_ref, beta_ref, y_ref, *, n_chunks):
    z = z_ref[...]
    z = 0.5 * z * (1.0 + jnp.tanh(math.sqrt(2.0 / math.pi) * (z + 0.044715 * (z * z * z))))
    u = z[:, :BRANCH_W]
    v = _layer_norm(z[:, BRANCH_W:], g_ref[...], beta_ref[...], LN_EPS).astype(BF16)
    causal = _tril_mask(SG_CHUNK)
    gw = BRANCH_W // SG_GROUPS
    for g in range(SG_GROUPS):
        wg = jnp.where(causal, w_ref[g], 0.0).astype(BF16)
        for c in range(n_chunks):
            rows = slice(c * SG_CHUNK, (c + 1) * SG_CHUNK)
            cols = slice(g * gw, (g + 1) * gw)
            s = jnp.dot(wg, v[rows, cols], preferred_element_type=F32) + b_ref[g]
            y_ref[rows, cols] = (u[rows, cols] * s).astype(BF16)


def sg_branch(zall, col_block, sg_w, sg_b, ln_g, ln_b, tm=512):
    t = zall.shape[0]
    tm = min(tm, t)
    return pl.pallas_call(
        functools.partial(_sg_kernel, n_chunks=tm // SG_CHUNK),
        grid=(t // tm,),
        in_specs=[pl.BlockSpec((tm, 2 * BRANCH_W), lambda i: (i, col_block)),
                  pl.BlockSpec((SG_GROUPS, SG_CHUNK, SG_CHUNK), lambda i: (0, 0, 0)),
                  pl.BlockSpec((SG_GROUPS, SG_CHUNK, 1), lambda i: (0, 0, 0)),
                  pl.BlockSpec((1, BRANCH_W), lambda i: (0, 0)),
                  pl.BlockSpec((1, BRANCH_W), lambda i: (0, 0))],
        out_specs=pl.BlockSpec((tm, BRANCH_W), lambda i: (i, 0)),
        out_shape=jax.ShapeDtypeStruct((t, BRANCH_W), BF16),
        compiler_params=_params("parallel"),
    )(zall, sg_w, sg_b.reshape(SG_GROUPS, SG_CHUNK, 1), ln_g.reshape(1, -1), ln_b.reshape(1, -1))


def _hgrn_kernel(z_ref, lb_ref, ng_ref, y_ref, st_ref, *, batch):
    c = HG_CHUNK

    @pl.when(pl.program_id(0) == 0)
    def _():
        st_ref[...] = jnp.zeros_like(st_ref)

    lb = lb_ref[...]
    ltri = jnp.where(_tril_mask(c), 1.0, 0.0).astype(BF16)
    rows = lax.broadcasted_iota(jnp.int32, (c, 1), 0)
    w = BRANCH_W
    for b in range(batch):
        zq = z_ref[b, :, 0:w]
        zf = z_ref[b, :, w:2 * w]
        zi = z_ref[b, :, 2 * w:3 * w]
        zg = z_ref[b, :, 3 * w:4 * w]
        log_f = _log_sigmoid(zf) + jnp.log1p(lb * jnp.exp(jnp.minimum(-zf, HG_EXP_CLIP)))
        log_f = jnp.minimum(log_f, 0.0)
        kx = (1.0 - lb) * _sigmoid(-zf)
        q = zq * _sigmoid(zq)
        bc = _dot_exact_lhs(ltri, log_f)
        b_last = bc[c - 1:c, :]
        qd = q * jnp.exp(bc)
        kd = kx * jnp.exp(b_last - bc)
        e_last = jnp.exp(b_last)
        outs = []
        for h in range(HG_HEADS):
            sl = slice(h * HG_DK, (h + 1) * HG_DK)
            st = st_ref[b * HG_HEADS + h]
            o = _dot_nt(qd[:, sl], st)
            qh, kh, bh, vh = q[:, sl], kx[:, sl], bc[:, sl], zi[:, sl]
            for s in range(c):
                m = qh * kh[s:s + 1, :] * jnp.exp(jnp.minimum(bh - bh[s:s + 1, :], 0.0))
                att = jnp.where(rows >= s, jnp.sum(m, axis=-1, keepdims=True), 0.0)
                o = o + att * vh[s:s + 1, :]
            st_ref[b * HG_HEADS + h] = st * e_last[:, sl] + _dot_tn(vh, kd[:, sl])
            outs.append(o)
        o = jnp.concatenate(outs, axis=-1) * _sigmoid(zg)
        y = o * lax.rsqrt(jnp.mean(o * o, axis=-1, keepdims=True) + HG_EPS) * ng_ref[...]
        y_ref[b] = y.astype(BF16)


def hgrn2_branch(z3, col_block, lb, norm_g):
    batch, seq, _ = z3.shape
    return pl.pallas_call(
        functools.partial(_hgrn_kernel, batch=batch),
        grid=(seq // HG_CHUNK,),
        in_specs=[pl.BlockSpec((batch, HG_CHUNK, 4 * BRANCH_W), lambda c: (0, c, col_block)),
                  pl.BlockSpec((1, BRANCH_W), lambda c: (0, 0)),
                  pl.BlockSpec((1, BRANCH_W), lambda c: (0, 0))],
        out_specs=pl.BlockSpec((batch, HG_CHUNK, BRANCH_W), lambda c: (0, c, 0)),
        out_shape=jax.ShapeDtypeStruct((batch, seq, BRANCH_W), BF16),
        scratch_shapes=[pltpu.VMEM((batch * HG_HEADS, HG_DK, HG_DK), F32)],
        compiler_params=_params("arbitrary"),
    )(z3, lb.reshape(1, -1), norm_g.reshape(1, -1))


def _rw_prep_kernel(*refs, tiles_per_seq, has_vmix):
    if has_vmix:
        (z_ref, zp_ref, mu_ref, w0_ref, w2_ref, a0_ref, a2_ref, g2_ref, kk_ref, ka_ref, ones_ref,
         v0_ref, v1_ref, v2_ref, vf_ref, r_o, lw_o, k_o, v_o, kk_o, al_o, g_o) = refs
    else:
        (z_ref, zp_ref, mu_ref, w0_ref, w2_ref, a0_ref, a2_ref, g2_ref, kk_ref, ka_ref, ones_ref,
         r_o, lw_o, k_o, v_o, kk_o, al_o, g_o) = refs
    z = z_ref[...]
    tm = z.shape[0]
    first = (pl.program_id(0) % tiles_per_seq) == 0
    prev = jnp.where(first, 0.0, zp_ref[7:8, :])
    rows = lax.broadcasted_iota(jnp.int32, (tm, 1), 0)
    zs = jnp.where(rows == 0, prev, pltpu.roll(z, 1, 0))
    zc = z + (zs - z) * mu_ref[...]
    w = BRANCH_W
    p = RW_LORA_PAD
    r, k, v = zc[:, 0:w], zc[:, w:2 * w], zc[:, 2 * w:3 * w]
    wl, al_in, gl = zc[:, 3 * w:3 * w + p], zc[:, 3 * w + p:3 * w + 2 * p], zc[:, 3 * w + 2 * p:]
    u = w0_ref[...] + _dot(jnp.tanh(wl), w2_ref[...])
    lw = -math.exp(-0.5) * _sigmoid(u)
    a = _sigmoid(a0_ref[...] + _dot(al_in, a2_ref[...]))
    g = _dot(_sigmoid(gl), g2_ref[...])
    if has_vmix:
        mix = _sigmoid(v0_ref[...] + _dot(_dot(v, v1_ref[...]), v2_ref[...]))
        v = v + (vf_ref[...] - v) * mix
    kkx = k * kk_ref[...]
    norm = jnp.sqrt(_head_sum(kkx * kkx, ones_ref[...]))
    kkn = kkx / jnp.maximum(norm, 1e-12)
    r_o[...] = r
    lw_o[...] = lw
    k_o[...] = k * (1.0 + (a - 1.0) * ka_ref[...])
    v_o[...] = v
    kk_o[...] = kkn
    al_o[...] = a
    g_o[...] = g


def rw_prep(zall, col_block, seq, mu, w0, w2p, a0, a2p, g2, k_k, k_a, ones_blk, vmix, v_first, tm=256):
    t = zall.shape[0]
    tm = min(tm, seq)
    zw = 4 * BRANCH_W
    row = lambda i: (i, 0)
    const = lambda i: (0, 0)
    vec = pl.BlockSpec((1, BRANCH_W), const)
    in_specs = [pl.BlockSpec((tm, zw), lambda i: (i, col_block)),
                pl.BlockSpec((8, zw), lambda i: (jnp.maximum(i * (tm // 8) - 1, 0), col_block)),
                pl.BlockSpec((1, zw), const), vec,
                pl.BlockSpec((RW_LORA_PAD, BRANCH_W), const), vec,
                pl.BlockSpec((RW_LORA_PAD, BRANCH_W), const),
                pl.BlockSpec((RW_G_LORA, BRANCH_W), const), vec, vec,
                pl.BlockSpec((LANES, LANES), const)]
    args = [zall, zall, mu.reshape(1, -1), w0.reshape(1, -1), w2p, a0.reshape(1, -1), a2p, g2,
            k_k.reshape(1, -1), k_a.reshape(1, -1), ones_blk]
    if vmix is not None:
        v0, v1, v2 = vmix
        in_specs += [vec, pl.BlockSpec(v1.shape, const), pl.BlockSpec(v2.shape, const),
                     pl.BlockSpec((tm, BRANCH_W), row)]
        args += [v0.reshape(1, -1), v1, v2, v_first]
    out = jax.ShapeDtypeStruct((t, BRANCH_W), F32)
    return pl.pallas_call(
        functools.partial(_rw_prep_kernel, tiles_per_seq=seq // tm, has_vmix=vmix is not None),
        grid=(t // tm,),
        in_specs=in_specs,
        out_specs=[pl.BlockSpec((tm, BRANCH_W), row)] * 7,
        out_shape=[out] * 7,
        compiler_params=_params("parallel"),
    )(*args)


def _rwkv_kernel(r_ref, lw_ref, k_ref, v_ref, kk_ref, al_ref, g_ref, gng_ref, gnb_ref, rk_ref, ones_ref,
                 y_ref, st_ref, *, batch):
    c = RW_CHUNK

    @pl.when(pl.program_id(0) == 0)
    def _():
        st_ref[...] = jnp.zeros_like(st_ref)

    incl = _tril_mask(c)
    strict = _tril_mask(c, strict=True)
    ltri = jnp.where(incl, 1.0, 0.0).astype(BF16)
    units = [(b, h) for b in range(batch) for h in range(RW_HEADS)]
    ar, bk, vv, st, g_last = {}, {}, {}, {}, {}
    for b in range(batch):
        lw, kk = lw_ref[b], kk_ref[b]
        cum = _dot_exact_lhs(ltri, lw)
        gam = jnp.exp(cum)
        ginv = jnp.exp(-cum)
        at = -kk * jnp.exp(cum - lw)
        rt = r_ref[b] * gam
        bt = kk * al_ref[b] * ginv
        kt = k_ref[b] * ginv
        v = v_ref[b]
        for h in range(RW_HEADS):
            sl = slice(h * RW_HEAD, (h + 1) * RW_HEAD)
            ar[b, h] = jnp.concatenate([at[:, sl], rt[:, sl]], axis=0).astype(BF16)
            bk[b, h] = jnp.concatenate([bt[:, sl], kt[:, sl]], axis=0).astype(BF16)
            vv[b, h] = v[:, sl]
            st[b, h] = st_ref[b * RW_HEADS + h]
            g_last[b, h] = gam[c - 1:c, sl]
    nb = {u: _dot_nt(ar[u], bk[u][:c]) for u in units}
    nk = {u: _dot_nt(ar[u], bk[u][c:]) for u in units}
    a_s = {u: _dot_nt(ar[u], st[u]) for u in units}
    n = {u: jnp.where(strict, nb[u][:c], 0.0).astype(BF16) for u in units}
    rb = {u: jnp.where(incl, nb[u][c:], 0.0) for u in units}
    akrk = {u: jnp.concatenate([jnp.where(strict, nk[u][:c], 0.0), jnp.where(incl, nk[u][c:], 0.0)], axis=0)
            for u in units}
    kv = {u: _dot(akrk[u], vv[u]) for u in units}
    x = {u: a_s[u][:c] + kv[u][:c] for u in units}
    uu = {u: x[u] + _dot(n[u], x[u]) for u in units}
    p = n
    for _ in range(int(math.log2(c)) - 1):
        p = {u: _dot(p[u], p[u]).astype(BF16) for u in units}
        uu = {u: uu[u] + _dot(p[u], uu[u]) for u in units}
    y = {u: a_s[u][c:] + kv[u][c:] + _dot(rb[u], uu[u]) for u in units}
    for u in units:
        upd = _dot_tn(jnp.concatenate([uu[u], vv[u]], axis=0), bk[u])
        st_ref[u[0] * RW_HEADS + u[1]] = (st[u] + upd) * g_last[u]
    ones = ones_ref[...]
    inv_n = 1.0 / RW_HEAD
    for b in range(batch):
        yb = jnp.concatenate([y[b, h] for h in range(RW_HEADS)], axis=-1)
        mu_y = _head_sum(yb, ones) * inv_n
        yc = yb - mu_y
        var = _head_sum(yc * yc, ones) * inv_n
        yn = yc * lax.rsqrt(var + RW_GN_EPS) * gng_ref[...] + gnb_ref[...]
        bonus = _head_sum(r_ref[b] * k_ref[b] * rk_ref[...], ones)
        y_ref[b] = ((yn + bonus * v_ref[b]) * g_ref[b]).astype(BF16)


def rwkv7_mix(r, lw, k, v, kk, al, g, gn_g, gn_b, r_k, ones_blk, batch, seq):
    arrs = [x.reshape(batch, seq, BRANCH_W) for x in (r, lw, k, v, kk, al, g)]
    blk = pl.BlockSpec((batch, RW_CHUNK, BRANCH_W), lambda c: (0, c, 0))
    vec = pl.BlockSpec((1, BRANCH_W), lambda c: (0, 0))
    return pl.pallas_call(
        functools.partial(_rwkv_kernel, batch=batch),
        grid=(seq // RW_CHUNK,),
        in_specs=[blk] * 7 + [vec] * 3 + [pl.BlockSpec((LANES, LANES), lambda c: (0, 0))],
        out_specs=blk,
        out_shape=jax.ShapeDtypeStruct((batch, seq, BRANCH_W), BF16),
        scratch_shapes=[pltpu.VMEM((batch * RW_HEADS, RW_HEAD, RW_HEAD), F32)],
        compiler_params=_params("arbitrary"),
    )(*arrs, gn_g.reshape(1, -1), gn_b.reshape(1, -1), r_k.reshape(1, -1), ones_blk)


def _merge_kernel(x_ref, ysg_ref, yhg_ref, yrw_ref, wg_ref, wb_ref, o_ref):
    x = x_ref[...]
    acc = None
    for b, y_ref in enumerate((ysg_ref, yhg_ref, yrw_ref)):
        gate = _sigmoid(jnp.dot(x, wg_ref[b], preferred_element_type=F32))
        term = gate * jnp.dot(y_ref[...], wb_ref[b], preferred_element_type=F32)
        acc = term if acc is None else acc + term
    o_ref[...] = acc.astype(BF16)


def merge_branches(xb, y_sg, y_hg, y_rw, w_gate, w_branch, tm=1024, tn=512):
    t, d = xb.shape
    tm = min(tm, t)
    ysp = pl.BlockSpec((tm, BRANCH_W), lambda i, j: (i, 0))
    return pl.pallas_call(
        _merge_kernel,
        grid=(t // tm, d // tn),
        in_specs=[pl.BlockSpec((tm, d), lambda i, j: (i, 0)), ysp, ysp, ysp,
                  pl.BlockSpec((3, d, tn), lambda i, j: (0, 0, j)),
                  pl.BlockSpec((3, BRANCH_W, tn), lambda i, j: (0, 0, j))],
        out_specs=pl.BlockSpec((tm, tn), lambda i, j: (i, j)),
        out_shape=jax.ShapeDtypeStruct((t, d), BF16),
        compiler_params=_params("parallel", "parallel"),
    )(xb, y_sg, y_hg, y_rw, w_gate, w_branch)


def _xattn_kernel(xb_ref, xf_ref, k_ref, v_ref, wq_ref, wo_ref, g_ref, b_ref, of_ref, ob_ref, *, alpha):
    q = jnp.dot(xb_ref[0], wq_ref[...], preferred_element_type=F32)
    kmem, vmem = k_ref[0], v_ref[0]
    scale = XA_HEAD_DIM ** -0.5
    outs = []
    for h in range(XA_HEADS):
        sl = slice(h * XA_HEAD_DIM, (h + 1) * XA_HEAD_DIM)
        s = _dot_nt(q[:, sl], kmem[:, sl]) * scale
        e = jnp.exp(s - jnp.max(s, axis=-1, keepdims=True))
        p = e / jnp.sum(e, axis=-1, keepdims=True)
        outs.append(_dot(p, vmem[:, sl]))
    o = jnp.concatenate(outs, axis=-1)
    h_out = _dot(o, wo_ref[...])
    y = _layer_norm(alpha * xf_ref[0] + h_out, g_ref[...], b_ref[...], LN_EPS)
    of_ref[0] = y
    ob_ref[0] = y.astype(BF16)


def cross_attention_ln(xb, xf, kmem, vmem, wq, wo, g, b, alpha, tm=512):
    batch, seq, d = xf.shape
    m, xw = kmem.shape[1], kmem.shape[2]
    tm = min(tm, seq)
    xs = pl.BlockSpec((1, tm, d), lambda bi, i: (bi, i, 0))
    ms = pl.BlockSpec((1, m, xw), lambda bi, i: (bi, 0, 0))
    vec = pl.BlockSpec((1, d), lambda bi, i: (0, 0))
    return pl.pallas_call(
        functools.partial(_xattn_kernel, alpha=alpha),
        grid=(batch, seq // tm),
        in_specs=[xs, xs, ms, ms,
                  pl.BlockSpec((d, xw), lambda bi, i: (0, 0)),
                  pl.BlockSpec((xw, d), lambda bi, i: (0, 0)), vec, vec],
        out_specs=[xs, xs],
        out_shape=[jax.ShapeDtypeStruct((batch, seq, d), F32), jax.ShapeDtypeStruct((batch, seq, d), BF16)],
        compiler_params=_params("parallel", "parallel"),
    )(xb, xf, kmem, vmem, wq, wo, g.reshape(1, d), b.reshape(1, d))


def _ffn_kernel(be_ref, x_ref, w1_ref, w3_ref, w2_ref, g_ref, b_ref, *rest, alpha, final_ln):
    del be_ref
    if final_ln:
        of_ref, ob_ref, xb_scr, acc_scr = rest
    else:
        of_ref, xb_scr, acc_scr = rest
    f = pl.program_id(1)

    @pl.when(f == 0)
    def _():
        xb_scr[...] = x_ref[...].astype(BF16)

    xb = xb_scr[...]
    gate = jnp.dot(xb, w1_ref[0], preferred_element_type=F32)
    up = jnp.dot(xb, w3_ref[0], preferred_element_type=F32)
    hh = (gate * _sigmoid(gate) * up).astype(BF16)
    contrib = jnp.dot(hh, w2_ref[0], preferred_element_type=F32)

    @pl.when(f == 0)
    def _():
        acc_scr[...] = contrib

    @pl.when(f > 0)
    def _():
        acc_scr[...] += contrib

    @pl.when(f == pl.num_programs(1) - 1)
    def _():
        if final_ln:
            y = _layer_norm(alpha * x_ref[...] + acc_scr[...], g_ref[...], b_ref[...], LN_EPS)
            of_ref[...] = y
            ob_ref[...] = y.astype(BF16)
        else:
            of_ref[...] = acc_scr[...]


def swiglu(x, block_expert, w1, w3, w2, g, b, alpha, final_ln, tm=512, tf=512):
    rows, d = x.shape
    ff = w1.shape[2]
    tm = min(tm, rows)
    xs = pl.BlockSpec((tm, d), lambda i, f, be: (i, 0))
    vec = pl.BlockSpec((1, d), lambda i, f, be: (0, 0))
    n_out = 2 if final_ln else 1
    out_shape = [jax.ShapeDtypeStruct((rows, d), F32), jax.ShapeDtypeStruct((rows, d), BF16)][:n_out]
    res = pl.pallas_call(
        functools.partial(_ffn_kernel, alpha=alpha, final_ln=final_ln),
        grid_spec=pltpu.PrefetchScalarGridSpec(
            num_scalar_prefetch=1,
            grid=(rows // tm, ff // tf),
            in_specs=[xs,
                      pl.BlockSpec((1, d, tf), lambda i, f, be: (be[i], 0, f)),
                      pl.BlockSpec((1, d, tf), lambda i, f, be: (be[i], 0, f)),
                      pl.BlockSpec((1, tf, d), lambda i, f, be: (be[i], f, 0)),
                      vec, vec],
            out_specs=[xs] * n_out,
            scratch_shapes=[pltpu.VMEM((tm, d), BF16), pltpu.VMEM((tm, d), F32)]),
        out_shape=out_shape,
        compiler_params=_params("parallel", "arbitrary"),
    )(block_expert, x, w1, w3, w2, g.reshape(1, d), b.reshape(1, d))
    return res if final_ln else res[0]


def _router_kernel(x_ref, w_ref, o_ref):
    x = x_ref[...]
    hi, mid, lo = _split3(x)
    whi, wmid, wlo = _split3(w_ref[...])
    dot = lambda a, b: jnp.dot(a, b, preferred_element_type=F32)
    logits = (dot(hi, whi) + (dot(hi, wmid) + dot(mid, whi))
              + (dot(mid, wmid) + dot(hi, wlo) + dot(lo, whi)))
    lane = lax.broadcasted_iota(jnp.int32, logits.shape, 1)
    neg = -jnp.inf
    lg = jnp.where(lane < N_EXPERTS, logits, neg)
    m1 = jnp.max(lg, axis=-1, keepdims=True)
    i1 = jnp.min(jnp.where(lg == m1, lane, LANES), axis=-1, keepdims=True)
    lg2 = jnp.where(lane == i1, neg, lg)
    m2 = jnp.max(lg2, axis=-1, keepdims=True)
    i2 = jnp.min(jnp.where(lg2 == m2, lane, LANES), axis=-1, keepdims=True)
    e = jnp.exp(m2 - m1)
    g1 = 1.0 / (1.0 + e)
    g2 = e / (1.0 + e)
    out = jnp.where(lane == 0, i1.astype(F32),
                    jnp.where(lane == 1, i2.astype(F32),
                              jnp.where(lane == 2, g1, jnp.where(lane == 3, g2, 0.0))))
    o_ref[...] = out


def route_top2(x, w_router_pad, tm=1024):
    t, d = x.shape
    tm = min(tm, t)
    return pl.pallas_call(
        _router_kernel,
        grid=(t // tm,),
        in_specs=[pl.BlockSpec((tm, d), lambda i: (i, 0)), pl.BlockSpec((d, LANES), lambda i: (0, 0))],
        out_specs=pl.BlockSpec((tm, LANES), lambda i: (i, 0)),
        out_shape=jax.ShapeDtypeStruct((t, LANES), F32),
        compiler_params=_params("parallel"),
    )(x, w_router_pad)


def _row_copy(src_hbm, dst_ref, sem, src_row, dst_row):
    return pltpu.make_async_copy(src_hbm.at[pl.ds(src_row, 1)], dst_ref.at[pl.ds(dst_row, 1)], sem)


def _gather_rows_kernel(idx_ref, src_hbm, dst_ref, sem, *, rows):
    def start(r, carry):
        _row_copy(src_hbm, dst_ref, sem, idx_ref[0, 0, r], r).start()
        return carry

    def wait(r, carry):
        _row_copy(src_hbm, dst_ref, sem, 0, r).wait()
        return carry

    lax.fori_loop(0, rows, start, 0)
    lax.fori_loop(0, rows, wait, 0)


def gather_rows(src, idx, rows=512):
    n = idx.shape[0]
    d = src.shape[1]
    return pl.pallas_call(
        functools.partial(_gather_rows_kernel, rows=rows),
        grid=(n // rows,),
        in_specs=[pl.BlockSpec((1, 1, rows), lambda i: (i, 0, 0), memory_space=pltpu.SMEM),
                  pl.BlockSpec(memory_space=pl.ANY)],
        out_specs=pl.BlockSpec((rows, d), lambda i: (i, 0)),
        out_shape=jax.ShapeDtypeStruct((n, d), src.dtype),
        scratch_shapes=[pltpu.SemaphoreType.DMA(())],
        compiler_params=_params("arbitrary"),
    )(idx.reshape(n // rows, 1, rows), src)


def _combine_kernel(idx_ref, y_hbm, x_ref, gate_ref, g_ref, b_ref, of_ref, ob_ref, ybuf, sem, *, alpha):
    tm = x_ref.shape[0]

    def start(r, carry):
        _row_copy(y_hbm, ybuf, sem, idx_ref[0, 0, r], r).start()
        return carry

    def wait(r, carry):
        _row_copy(y_hbm, ybuf, sem, 0, r).wait()
        return carry

    lax.fori_loop(0, 2 * tm, start, 0)
    lax.fori_loop(0, 2 * tm, wait, 0)
    gates = gate_ref[...]
    h = gates[:, 0:1] * ybuf[0:tm, :] + gates[:, 1:2] * ybuf[tm:2 * tm, :]
    y = _layer_norm(alpha * x_ref[...] + h, g_ref[...], b_ref[...], LN_EPS)
    of_ref[...] = y
    ob_ref[...] = y.astype(BF16)


def moe_combine_ln(y_slots, dest, gates, x, g, b, alpha, tm=256):
    t, d = x.shape
    tm = min(tm, t)
    idx = dest.reshape(t // tm, tm, 2).transpose(0, 2, 1).reshape(t // tm, 1, 2 * tm)
    row = lambda i: (i, 0)
    vec = pl.BlockSpec((1, d), lambda i: (0, 0))
    return pl.pallas_call(
        functools.partial(_combine_kernel, alpha=alpha),
        grid=(t // tm,),
        in_specs=[pl.BlockSpec((1, 1, 2 * tm), lambda i: (i, 0, 0), memory_space=pltpu.SMEM),
                  pl.BlockSpec(memory_space=pl.ANY),
                  pl.BlockSpec((tm, d), row), pl.BlockSpec((tm, 2), row), vec, vec],
        out_specs=[pl.BlockSpec((tm, d), row), pl.BlockSpec((tm, d), row)],
        out_shape=[jax.ShapeDtypeStruct((t, d), F32), jax.ShapeDtypeStruct((t, d), BF16)],
        scratch_shapes=[pltpu.VMEM((2 * tm, d), F32), pltpu.SemaphoreType.DMA(())],
        compiler_params=_params("arbitrary"),
    )(idx, y_slots, x, gates, g.reshape(1, d), b.reshape(1, d))


def moe_swiglu_ln(xf, w_router_pad, w1, w3, w2, g, b, alpha):
    t, d = xf.shape
    routed = route_top2(xf, w_router_pad)
    expert = routed[:, 0:2].astype(jnp.int32)
    gates = routed[:, 2:4]
    flat_e = expert.reshape(-1)
    onehot = (flat_e[:, None] == jnp.arange(N_EXPERTS, dtype=jnp.int32)[None, :]).astype(jnp.int32)
    rank = jnp.sum((jnp.cumsum(onehot, axis=0) - 1) * onehot, axis=1)
    counts = jnp.sum(onehot, axis=0)
    padded = (counts + MOE_ROWS - 1) // MOE_ROWS * MOE_ROWS
    ends = jnp.cumsum(padded)
    dest = (ends - padded)[flat_e] + rank
    n_blocks = (t * 2) // MOE_ROWS + N_EXPERTS
    n_slots = n_blocks * MOE_ROWS
    flat_tok = jnp.arange(t * 2, dtype=jnp.int32) // 2
    slot_tok = jnp.zeros((n_slots,), jnp.int32).at[dest].set(flat_tok)
    block_start = jnp.arange(n_blocks, dtype=jnp.int32) * MOE_ROWS
    block_expert = jnp.minimum(jnp.sum((ends[None, :] <= block_start[:, None]).astype(jnp.int32), axis=1),
                               N_EXPERTS - 1)
    x_slots = gather_rows(xf, slot_tok, rows=MOE_ROWS)
    y_slots = swiglu(x_slots, block_expert, w1, w3, w2, g, b, alpha, final_ln=False, tm=MOE_ROWS)
    return moe_combine_ln(y_slots, dest.reshape(t, 2).astype(jnp.int32), gates, xf, g, b, alpha)


def kernel(x, mem, w_in, sg_w, sg_b, sg_ln_g, sg_ln_b, hg_lb_logits, hg_norm_g, rw_mu, rw_w0, rw_w2, rw_a0, rw_a2, rw_g2, rw_k_k, rw_k_a, rw_r_k, rw_gn_g, rw_gn_b, rw_v0, rw_v1, rw_v2, w_branch, w_mix_out, xa_wq, xa_wk, xa_wv, xa_wo, ln_g, ln_b, ffn_w1, ffn_w3, ffn_w2, moe_router, moe_w1, moe_w3, moe_w2):
    batch, seq, d = x.shape
    depth = w_in.shape[0]
    t = batch * seq
    alpha = (2 * depth) ** 0.25
    w = BRANCH_W
    sg_cols, hg_cols = 2 * w, 4 * w
    rw_cols = 3 * w + RW_W_LORA + RW_A_LORA + RW_G_LORA
    o_hg, o_rw, o_gate = sg_cols, sg_cols + hg_cols, sg_cols + hg_cols + rw_cols

    p = jax.nn.softmax(hg_lb_logits.astype(F32), axis=0)
    lower_bounds = jnp.cumsum(p, axis=0) - p[0]
    lane_head = jnp.arange(LANES) // RW_HEAD
    ones_blk = (lane_head[:, None] == lane_head[None, :]).astype(BF16)
    zero_block = jnp.zeros((1,), jnp.int32)

    def pad_cols(m, n):
        return jnp.pad(m, ((0, 0), (0, n - m.shape[1])))

    def pad_rows(m, n):
        return jnp.pad(m, ((0, n - m.shape[0]), (0, 0)))

    mem_b = mem.reshape(-1, d).astype(BF16)
    xf = x.reshape(t, d)
    xb = xf.astype(BF16)
    v_first = None
    for l in range(depth):
        wl_ = w_in[l]
        c0 = o_rw + 3 * w
        w_rw = jnp.concatenate([wl_[:, o_rw:c0],
                                pad_cols(wl_[:, c0:c0 + RW_W_LORA], RW_LORA_PAD),
                                pad_cols(wl_[:, c0 + RW_W_LORA:c0 + RW_W_LORA + RW_A_LORA], RW_LORA_PAD),
                                wl_[:, c0 + RW_W_LORA + RW_A_LORA:o_gate]], axis=1)
        w_proj = jnp.concatenate([wl_[:, o_hg:o_rw], w_rw, wl_[:, :o_hg]], axis=1).astype(BF16)
        w_gate = wl_[:, o_gate:].reshape(d, 3, d).transpose(1, 0, 2).astype(BF16)
        mu = rw_mu[l]
        mu_p = jnp.concatenate([mu[:3 * w],
                                jnp.pad(mu[3 * w:3 * w + RW_W_LORA], (0, RW_LORA_PAD - RW_W_LORA)),
                                jnp.pad(mu[3 * w + RW_W_LORA:3 * w + RW_W_LORA + RW_A_LORA],
                                        (0, RW_LORA_PAD - RW_A_LORA)),
                                mu[3 * w + RW_W_LORA + RW_A_LORA:]])

        zall = matmul(xb, w_proj, F32)
        y_sg = sg_branch(zall, 4, sg_w[l], sg_b[l], sg_ln_g[l], sg_ln_b[l])
        y_hg = hgrn2_branch(zall.reshape(batch, seq, -1), 0, lower_bounds[l], hg_norm_g[l]).reshape(t, w)
        vmix = None if l == 0 else (rw_v0[l - 1], rw_v1[l - 1].astype(BF16), rw_v2[l - 1].astype(BF16))
        r_, lw_, k_, v_, kk_, al_, g_ = rw_prep(
            zall, 1, seq, mu_p, rw_w0[l], pad_rows(rw_w2[l], RW_LORA_PAD).astype(BF16), rw_a0[l],
            pad_rows(rw_a2[l], RW_LORA_PAD).astype(BF16), rw_g2[l].astype(BF16), rw_k_k[l], rw_k_a[l],
            ones_blk, vmix, v_first)
        if l == 0:
            v_first = v_
        y_rw = rwkv7_mix(r_, lw_, k_, v_, kk_, al_, g_, rw_gn_g[l], rw_gn_b[l], rw_r_k[l], ones_blk,
                         batch, seq).reshape(t, w)
        merged = merge_branches(xb, y_sg, y_hg, y_rw, w_gate, w_branch[l].astype(BF16))
        xf, xb = matmul_res_ln(merged, w_mix_out[l].astype(BF16), xf, ln_g[l, 0], ln_b[l, 0], alpha)

        kmem = matmul(mem_b, xa_wk[l].astype(BF16), BF16).reshape(batch, -1, XA_HEADS * XA_HEAD_DIM)
        vmem = matmul(mem_b, xa_wv[l].astype(BF16), BF16).reshape(batch, -1, XA_HEADS * XA_HEAD_DIM)
        xf3, xb3 = cross_attention_ln(xb.reshape(batch, seq, d), xf.reshape(batch, seq, d), kmem, vmem,
                                      xa_wq[l].astype(BF16), xa_wo[l].astype(BF16), ln_g[l, 1], ln_b[l, 1], alpha)
        xf, xb = xf3.reshape(t, d), xb3.reshape(t, d)

        if l % 2 == 0:
            i = l // 2
            xf, xb = swiglu(xf, zero_block.repeat(t // min(512, t)), ffn_w1[i:i + 1].astype(BF16),
                            ffn_w3[i:i + 1].astype(BF16), ffn_w2[i:i + 1].astype(BF16),
                            ln_g[l, 2], ln_b[l, 2], alpha, final_ln=True)
        else:
            i = l // 2
            xf, xb = moe_swiglu_ln(xf, pad_cols(moe_router[i], LANES), moe_w1[i].astype(BF16),
                                   moe_w3[i].astype(BF16), moe_w2[i].astype(BF16), ln_g[l, 2], ln_b[l, 2], alpha)
    return xf.reshape(batch, seq, d)
```

```python
import functools
import math

import jax
import jax.numpy as jnp
from jax import lax
from jax.experimental import pallas as pl
from jax.experimental.pallas import tpu as pltpu

F32 = jnp.float32
BF16 = jnp.bfloat16

V7X_VMEM_BYTES = 64 * 1024 * 1024
VMEM_LIMIT = V7X_VMEM_BYTES - 8 * 1024 * 1024
LANES = 128
SUBLANES = 8

BRANCH_W = 512
SG_CHUNK = 128
SG_GROUPS = 4
HG_HEADS = 4
HG_DK = BRANCH_W // HG_HEADS
HG_CHUNK = 16
HG_EPS = 1e-6
HG_EXP_CLIP = 60.0
RW_HEAD = 64
RW_HEADS = BRANCH_W // RW_HEAD
RW_CHUNK = 64
RW_W_LORA = 96
RW_A_LORA = 96
RW_G_LORA = 256
RW_GN_EPS = 64e-5
RW_LORA_PAD = 128
XA_HEADS = 4
XA_HEAD_DIM = 128
N_EXPERTS = 8
MOE_ROWS = 512
LN_EPS = 1e-5
DMA_UNROLL = 8
DMA_PRIORITIES = 2


def _params(*sem):
    return pltpu.CompilerParams(dimension_semantics=sem, vmem_limit_bytes=VMEM_LIMIT)


def _dot(a, b):
    return jnp.dot(a.astype(BF16), b.astype(BF16), preferred_element_type=F32)


def _dot_nt(a, b):
    return lax.dot_general(a.astype(BF16), b.astype(BF16), (((1,), (1,)), ((), ())),
                           preferred_element_type=F32)


def _dot_tn(a, b):
    return lax.dot_general(a.astype(BF16), b.astype(BF16), (((0,), (0,)), ((), ())),
                           preferred_element_type=F32)


def _split3(x):
    hi = x.astype(BF16)
    r1 = x - hi.astype(F32)
    mid = r1.astype(BF16)
    lo = (r1 - mid.astype(F32)).astype(BF16)
    return hi, mid, lo


def _dot_exact_lhs(m_bf16, x):
    hi, mid, lo = _split3(x)
    return (jnp.dot(m_bf16, hi, preferred_element_type=F32)
            + jnp.dot(m_bf16, mid, preferred_element_type=F32)
            + jnp.dot(m_bf16, lo, preferred_element_type=F32))


def _dot_exact_rhs(x, m_bf16):
    hi, mid, lo = _split3(x)
    return (jnp.dot(hi, m_bf16, preferred_element_type=F32)
            + jnp.dot(mid, m_bf16, preferred_element_type=F32)
            + jnp.dot(lo, m_bf16, preferred_element_type=F32))


def _head_sum(x, ones_blk):
    n = x.shape[-1] // LANES
    return jnp.concatenate(
        [_dot_exact_rhs(x[:, i * LANES:(i + 1) * LANES], ones_blk) for i in range(n)], axis=-1)


def _sigmoid(x):
    return 1.0 / (1.0 + jnp.exp(-x))


def _log_sigmoid(x):
    return jnp.minimum(x, 0.0) - jnp.log1p(jnp.exp(-jnp.abs(x)))


def _layer_norm(x, g, b, eps):
    mu = jnp.mean(x, axis=-1, keepdims=True)
    xc = x - mu
    var = jnp.mean(xc * xc, axis=-1, keepdims=True)
    return xc * lax.rsqrt(var + eps) * g + b


def _tril_mask(n, strict=False):
    row = lax.broadcasted_iota(jnp.int32, (n, n), 0)
    col = lax.broadcasted_iota(jnp.int32, (n, n), 1)
    return (row > col) if strict else (row >= col)


def _mm_kernel(a_ref, b_ref, o_ref):
    o_ref[...] = jnp.dot(a_ref[...], b_ref[...], preferred_element_type=F32).astype(o_ref.dtype)


def matmul(a, b, out_dtype, tm=1024, tn=512):
    m, k = a.shape
    n = b.shape[1]
    tm, tn = min(tm, m), min(tn, n)
    return pl.pallas_call(
        _mm_kernel,
        grid=(m // tm, n // tn),
        in_specs=[pl.BlockSpec((tm, k), lambda i, j: (i, 0)),
                  pl.BlockSpec((k, tn), lambda i, j: (0, j))],
        out_specs=pl.BlockSpec((tm, tn), lambda i, j: (i, j)),
        out_shape=jax.ShapeDtypeStruct((m, n), out_dtype),
        compiler_params=_params("parallel", "parallel"),
    )(a, b)


def _mm_res_ln_kernel(a_ref, w_ref, res_ref, g_ref, b_ref, of_ref, ob_ref, *, alpha):
    h = jnp.dot(a_ref[...], w_ref[...], preferred_element_type=F32)
    y = _layer_norm(alpha * res_ref[...] + h, g_ref[...], b_ref[...], LN_EPS)
    of_ref[...] = y
    ob_ref[...] = y.astype(BF16)


def matmul_res_ln(a, w, res, g, b, alpha, tm=512):
    m, k = a.shape
    n = w.shape[1]
    tm = min(tm, m)
    return pl.pallas_call(
        functools.partial(_mm_res_ln_kernel, alpha=alpha),
        grid=(m // tm,),
        in_specs=[pl.BlockSpec((tm, k), lambda i: (i, 0)),
                  pl.BlockSpec((k, n), lambda i: (0, 0)),
                  pl.BlockSpec((tm, n), lambda i: (i, 0)),
                  pl.BlockSpec((1, n), lambda i: (0, 0)),
                  pl.BlockSpec((1, n), lambda i: (0, 0))],
        out_specs=[pl.BlockSpec((tm, n), lambda i: (i, 0)),
                   pl.BlockSpec((tm, n), lambda i: (i, 0))],
        out_shape=[jax.ShapeDtypeStruct((m, n), F32), jax.ShapeDtypeStruct((m, n), BF16)],
        compiler_params=_params("parallel"),
    )(a, w, res, g.reshape(1, n), b.reshape(1, n))


def _sg_kernel(z_ref, w_ref, b_ref, g_ref, beta_ref, y_ref, *, n_chunks):
    z = z_ref[...]
    z = 0.5 * z * (1.0 + jnp.tanh(math.sqrt(2.0 / math.pi) * (z + 0.044715 * (z * z * z))))
    u = z[:, :BRANCH_W]
    v = _layer_norm(z[:, BRANCH_W:], g_ref[...], beta_ref[...], LN_EPS).astype(BF16)
    causal = _tril_mask(SG_CHUNK)
    gw = BRANCH_W // SG_GROUPS
    for g in range(SG_GROUPS):
        wg = jnp.where(causal, w_ref[g], 0.0).astype(BF16)
        for c in range(n_chunks):
            rows = slice(c * SG_CHUNK, (c + 1) * SG_CHUNK)
            cols = slice(g * gw, (g + 1) * gw)
            s = jnp.dot(wg, v[rows, cols], preferred_element_type=F32) + b_ref[g]
            y_ref[rows, cols] = (u[rows, cols] * s).astype(BF16)


def sg_branch(zall, col_block, sg_w, sg_b, ln_g, ln_b, tm=512):
    t = zall.shape[0]
    tm = min(tm, t)
    return pl.pallas_call(
        functools.partial(_sg_kernel, n_chunks=tm // SG_CHUNK),
        grid=(t // tm,),
        in_specs=[pl.BlockSpec((tm, 2 * BRANCH_W), lambda i: (i, col_block)),
                  pl.BlockSpec((SG_GROUPS, SG_CHUNK, SG_CHUNK), lambda i: (0, 0, 0)),
                  pl.BlockSpec((SG_GROUPS, SG_CHUNK, 1), lambda i: (0, 0, 0)),
                  pl.BlockSpec((1, BRANCH_W), lambda i: (0, 0)),
                  pl.BlockSpec((1, BRANCH_W), lambda i: (0, 0))],
        out_specs=pl.BlockSpec((tm, BRANCH_W), lambda i: (i, 0)),
        out_shape=jax.ShapeDtypeStruct((t, BRANCH_W), BF16),
        compiler_params=_params("parallel"),
    )(zall, sg_w, sg_b.reshape(SG_GROUPS, SG_CHUNK, 1), ln_g.reshape(1, -1), ln_b.reshape(1, -1))


def _hgrn_kernel(z_ref, lb_ref, ng_ref, y_ref, st_ref, *, batch):
    c = HG_CHUNK

    @pl.when(pl.program_id(0) == 0)
    def _():
        st_ref[...] = jnp.zeros_like(st_ref)

    lb = lb_ref[...]
    ltri = jnp.where(_tril_mask(c), 1.0, 0.0).astype(BF16)
    rows = lax.broadcasted_iota(jnp.int32, (c, 1), 0)
    w = BRANCH_W
    for b in range(batch):
        zq = z_ref[b, :, 0:w]
        zf = z_ref[b, :, w:2 * w]
        zi = z_ref[b, :, 2 * w:3 * w]
        zg = z_ref[b, :, 3 * w:4 * w]
        log_f = _log_sigmoid(zf) + jnp.log1p(lb * jnp.exp(jnp.minimum(-zf, HG_EXP_CLIP)))
        log_f = jnp.minimum(log_f, 0.0)
        kx = (1.0 - lb) * _sigmoid(-zf)
        q = zq * _sigmoid(zq)
        bc = _dot_exact_lhs(ltri, log_f)
        b_last = bc[c - 1:c, :]
        qd = q * jnp.exp(bc)
        kd = kx * jnp.exp(b_last - bc)
        e_last = jnp.exp(b_last)
        outs = []
        for h in range(HG_HEADS):
            sl = slice(h * HG_DK, (h + 1) * HG_DK)
            st = st_ref[b * HG_HEADS + h]
            o = _dot_nt(qd[:, sl], st)
            qh, kh, bh, vh = q[:, sl], kx[:, sl], bc[:, sl], zi[:, sl]
            parts = []
            for r0 in range(0, c, SUBLANES):
                rs = slice(r0, r0 + SUBLANES)
                qg, bg, og, rg = qh[rs], bh[rs], o[rs], rows[rs]
                for s in range(r0 + SUBLANES):
                    m = qg * kh[s:s + 1, :] * jnp.exp(jnp.minimum(bg - bh[s:s + 1, :], 0.0))
                    att = jnp.sum(m, axis=-1, keepdims=True)
                    if s > r0:
                        att = jnp.where(rg >= s, att, 0.0)
                    og = og + att * vh[s:s + 1, :]
                parts.append(og)
            o = jnp.concatenate(parts, axis=0)
            st_ref[b * HG_HEADS + h] = st * e_last[:, sl] + _dot_tn(vh, kd[:, sl])
            outs.append(o)
        o = jnp.concatenate(outs, axis=-1) * _sigmoid(zg)
        y = o * lax.rsqrt(jnp.mean(o * o, axis=-1, keepdims=True) + HG_EPS) * ng_ref[...]
        y_ref[b] = y.astype(BF16)


def hgrn2_branch(z3, col_block, lb, norm_g):
    batch, seq, _ = z3.shape
    return pl.pallas_call(
        functools.partial(_hgrn_kernel, batch=batch),
        grid=(seq // HG_CHUNK,),
        in_specs=[pl.BlockSpec((batch, HG_CHUNK, 4 * BRANCH_W), lambda c: (0, c, col_block)),
                  pl.BlockSpec((1, BRANCH_W), lambda c: (0, 0)),
                  pl.BlockSpec((1, BRANCH_W), lambda c: (0, 0))],
        out_specs=pl.BlockSpec((batch, HG_CHUNK, BRANCH_W), lambda c: (0, c, 0)),
        out_shape=jax.ShapeDtypeStruct((batch, seq, BRANCH_W), BF16),
        scratch_shapes=[pltpu.VMEM((batch * HG_HEADS, HG_DK, HG_DK), F32)],
        compiler_params=_params("arbitrary"),
    )(z3, lb.reshape(1, -1), norm_g.reshape(1, -1))


def _rw_prep_kernel(*refs, tiles_per_seq, has_vmix):
    if has_vmix:
        (z_ref, zp_ref, mu_ref, w0_ref, w2_ref, a0_ref, a2_ref, g2_ref, kk_ref, ka_ref, ones_ref,
         v0_ref, v1_ref, v2_ref, vf_ref, r_o, lw_o, k_o, v_o, kk_o, al_o, g_o) = refs
    else:
        (z_ref, zp_ref, mu_ref, w0_ref, w2_ref, a0_ref, a2_ref, g2_ref, kk_ref, ka_ref, ones_ref,
         r_o, lw_o, k_o, v_o, kk_o, al_o, g_o) = refs
    z = z_ref[...]
    tm = z.shape[0]
    first = (pl.program_id(0) % tiles_per_seq) == 0
    prev = jnp.where(first, 0.0, zp_ref[7:8, :])
    rows = lax.broadcasted_iota(jnp.int32, (tm, 1), 0)
    zs = jnp.where(rows == 0, prev, pltpu.roll(z, 1, 0))
    zc = z + (zs - z) * mu_ref[...]
    w = BRANCH_W
    p = RW_LORA_PAD
    r, k, v = zc[:, 0:w], zc[:, w:2 * w], zc[:, 2 * w:3 * w]
    wl, al_in, gl = zc[:, 3 * w:3 * w + p], zc[:, 3 * w + p:3 * w + 2 * p], zc[:, 3 * w + 2 * p:]
    u = w0_ref[...] + _dot(jnp.tanh(wl), w2_ref[...])
    lw = -math.exp(-0.5) * _sigmoid(u)
    a = _sigmoid(a0_ref[...] + _dot(al_in, a2_ref[...]))
    g = _dot(_sigmoid(gl), g2_ref[...])
    if has_vmix:
        mix = _sigmoid(v0_ref[...] + _dot(_dot(v, v1_ref[...]), v2_ref[...]))
        v = v + (vf_ref[...] - v) * mix
    kkx = k * kk_ref[...]
    norm = jnp.sqrt(_head_sum(kkx * kkx, ones_ref[...]))
    kkn = kkx / jnp.maximum(norm, 1e-12)
    r_o[...] = r
    lw_o[...] = lw
    k_o[...] = k * (1.0 + (a - 1.0) * ka_ref[...])
    v_o[...] = v
    kk_o[...] = kkn
    al_o[...] = a
    g_o[...] = g


def rw_prep(zall, col_block, seq, mu, w0, w2p, a0, a2p, g2, k_k, k_a, ones_blk, vmix, v_first, tm=256):
    t = zall.shape[0]
    tm = min(tm, seq)
    zw = 4 * BRANCH_W
    row = lambda i: (i, 0)
    const = lambda i: (0, 0)
    vec = pl.BlockSpec((1, BRANCH_W), const)
    in_specs = [pl.BlockSpec((tm, zw), lambda i: (i, col_block)),
                pl.BlockSpec((8, zw), lambda i: (jnp.maximum(i * (tm // 8) - 1, 0), col_block)),
                pl.BlockSpec((1, zw), const), vec,
                pl.BlockSpec((RW_LORA_PAD, BRANCH_W), const), vec,
                pl.BlockSpec((RW_LORA_PAD, BRANCH_W), const),
                pl.BlockSpec((RW_G_LORA, BRANCH_W), const), vec, vec,
                pl.BlockSpec((LANES, LANES), const)]
    args = [zall, zall, mu.reshape(1, -1), w0.reshape(1, -1), w2p, a0.reshape(1, -1), a2p, g2,
            k_k.reshape(1, -1), k_a.reshape(1, -1), ones_blk]
    if vmix is not None:
        v0, v1, v2 = vmix
        in_specs += [vec, pl.BlockSpec(v1.shape, const), pl.BlockSpec(v2.shape, const),
                     pl.BlockSpec((tm, BRANCH_W), row)]
        args += [v0.reshape(1, -1), v1, v2, v_first]
    out = jax.ShapeDtypeStruct((t, BRANCH_W), F32)
    return pl.pallas_call(
        functools.partial(_rw_prep_kernel, tiles_per_seq=seq // tm, has_vmix=vmix is not None),
        grid=(t // tm,),
        in_specs=in_specs,
        out_specs=[pl.BlockSpec((tm, BRANCH_W), row)] * 7,
        out_shape=[out] * 7,
        compiler_params=_params("parallel"),
    )(*args)


def _rwkv_kernel(r_ref, lw_ref, k_ref, v_ref, kk_ref, al_ref, g_ref, gng_ref, gnb_ref, rk_ref, ones_ref,
                 y_ref, st_ref, *, batch):
    c = RW_CHUNK

    @pl.when(pl.program_id(0) == 0)
    def _():
        st_ref[...] = jnp.zeros_like(st_ref)

    incl = _tril_mask(c)
    strict = _tril_mask(c, strict=True)
    ltri = jnp.where(incl, 1.0, 0.0).astype(BF16)
    units = [(b, h) for b in range(batch) for h in range(RW_HEADS)]
    ar, bk, vv, st, g_last = {}, {}, {}, {}, {}
    for b in range(batch):
        lw, kk = lw_ref[b], kk_ref[b]
        cum = _dot_exact_lhs(ltri, lw)
        gam = jnp.exp(cum)
        ginv = jnp.exp(-cum)
        at = -kk * jnp.exp(cum - lw)
        rt = r_ref[b] * gam
        bt = kk * al_ref[b] * ginv
        kt = k_ref[b] * ginv
        v = v_ref[b]
        for h in range(RW_HEADS):
            sl = slice(h * RW_HEAD, (h + 1) * RW_HEAD)
            ar[b, h] = jnp.concatenate([at[:, sl], rt[:, sl]], axis=0).astype(BF16)
            bk[b, h] = jnp.concatenate([bt[:, sl], kt[:, sl]], axis=0).astype(BF16)
            vv[b, h] = v[:, sl]
            st[b, h] = st_ref[b * RW_HEADS + h]
            g_last[b, h] = gam[c - 1:c, sl]
    nb = {u: _dot_nt(ar[u], bk[u][:c]) for u in units}
    nk = {u: _dot_nt(ar[u], bk[u][c:]) for u in units}
    a_s = {u: _dot_nt(ar[u], st[u]) for u in units}
    n = {u: jnp.where(strict, nb[u][:c], 0.0).astype(BF16) for u in units}
    rb = {u: jnp.where(incl, nb[u][c:], 0.0) for u in units}
    akrk = {u: jnp.concatenate([jnp.where(strict, nk[u][:c], 0.0), jnp.where(incl, nk[u][c:], 0.0)], axis=0)
            for u in units}
    kv = {u: _dot(akrk[u], vv[u]) for u in units}
    x = {u: a_s[u][:c] + kv[u][:c] for u in units}
    uu = {u: x[u] + _dot(n[u], x[u]) for u in units}
    p = n
    for _ in range(int(math.log2(c)) - 1):
        p = {u: _dot(p[u], p[u]).astype(BF16) for u in units}
        uu = {u: uu[u] + _dot(p[u], uu[u]) for u in units}
    y = {u: a_s[u][c:] + kv[u][c:] + _dot(rb[u], uu[u]) for u in units}
    for u in units:
        upd = _dot_tn(jnp.concatenate([uu[u], vv[u]], axis=0), bk[u])
        st_ref[u[0] * RW_HEADS + u[1]] = (st[u] + upd) * g_last[u]
    ones = ones_ref[...]
    inv_n = 1.0 / RW_HEAD
    for b in range(batch):
        yb = jnp.concatenate([y[b, h] for h in range(RW_HEADS)], axis=-1)
        mu_y = _head_sum(yb, ones) * inv_n
        yc = yb - mu_y
        var = _head_sum(yc * yc, ones) * inv_n
        yn = yc * lax.rsqrt(var + RW_GN_EPS) * gng_ref[...] + gnb_ref[...]
        bonus = _head_sum(r_ref[b] * k_ref[b] * rk_ref[...], ones)
        y_ref[b] = ((yn + bonus * v_ref[b]) * g_ref[b]).astype(BF16)


def rwkv7_mix(r, lw, k, v, kk, al, g, gn_g, gn_b, r_k, ones_blk, batch, seq):
    arrs = [x.reshape(batch, seq, BRANCH_W) for x in (r, lw, k, v, kk, al, g)]
    blk = pl.BlockSpec((batch, RW_CHUNK, BRANCH_W), lambda c: (0, c, 0))
    vec = pl.BlockSpec((1, BRANCH_W), lambda c: (0, 0))
    return pl.pallas_call(
        functools.partial(_rwkv_kernel, batch=batch),
        grid=(seq // RW_CHUNK,),
        in_specs=[blk] * 7 + [vec] * 3 + [pl.BlockSpec((LANES, LANES), lambda c: (0, 0))],
        out_specs=blk,
        out_shape=jax.ShapeDtypeStruct((batch, seq, BRANCH_W), BF16),
        scratch_shapes=[pltpu.VMEM((batch * RW_HEADS, RW_HEAD, RW_HEAD), F32)],
        compiler_params=_params("arbitrary"),
    )(*arrs, gn_g.reshape(1, -1), gn_b.reshape(1, -1), r_k.reshape(1, -1), ones_blk)


def _merge_kernel(x_ref, ysg_ref, yhg_ref, yrw_ref, wg_ref, wb_ref, o_ref):
    x = x_ref[...]
    acc = None
    for b, y_ref in enumerate((ysg_ref, yhg_ref, yrw_ref)):
        gate = _sigmoid(jnp.dot(x, wg_ref[b], preferred_element_type=F32))
        term = gate * jnp.dot(y_ref[...], wb_ref[b], preferred_element_type=F32)
        acc = term if acc is None else acc + term
    o_ref[...] = acc.astype(BF16)


def merge_branches(xb, y_sg, y_hg, y_rw, w_gate, w_branch, tm=1024, tn=512):
    t, d = xb.shape
    tm = min(tm, t)
    ysp = pl.BlockSpec((tm, BRANCH_W), lambda i, j: (i, 0))
    return pl.pallas_call(
        _merge_kernel,
        grid=(t // tm, d // tn),
        in_specs=[pl.BlockSpec((tm, d), lambda i, j: (i, 0)), ysp, ysp, ysp,
                  pl.BlockSpec((3, d, tn), lambda i, j: (0, 0, j)),
                  pl.BlockSpec((3, BRANCH_W, tn), lambda i, j: (0, 0, j))],
        out_specs=pl.BlockSpec((tm, tn), lambda i, j: (i, j)),
        out_shape=jax.ShapeDtypeStruct((t, d), BF16),
        compiler_params=_params("parallel", "parallel"),
    )(xb, y_sg, y_hg, y_rw, w_gate, w_branch)


def _xattn_kernel(xb_ref, xf_ref, k_ref, v_ref, wq_ref, wo_ref, g_ref, b_ref, of_ref, ob_ref, *, alpha):
    q = jnp.dot(xb_ref[0], wq_ref[...], preferred_element_type=F32)
    kmem, vmem = k_ref[0], v_ref[0]
    scale = XA_HEAD_DIM ** -0.5
    outs = []
    for h in range(XA_HEADS):
        sl = slice(h * XA_HEAD_DIM, (h + 1) * XA_HEAD_DIM)
        s = _dot_nt(q[:, sl], kmem[:, sl]) * scale
        e = jnp.exp(s - jnp.max(s, axis=-1, keepdims=True))
        p = e / jnp.sum(e, axis=-1, keepdims=True)
        outs.append(_dot(p, vmem[:, sl]))
    o = jnp.concatenate(outs, axis=-1)
    h_out = _dot(o, wo_ref[...])
    y = _layer_norm(alpha * xf_ref[0] + h_out, g_ref[...], b_ref[...], LN_EPS)
    of_ref[0] = y
    ob_ref[0] = y.astype(BF16)


def cross_attention_ln(xb, xf, kmem, vmem, wq, wo, g, b, alpha, tm=512):
    batch, seq, d = xf.shape
    m, xw = kmem.shape[1], kmem.shape[2]
    tm = min(tm, seq)
    xs = pl.BlockSpec((1, tm, d), lambda bi, i: (bi, i, 0))
    ms = pl.BlockSpec((1, m, xw), lambda bi, i: (bi, 0, 0))
    vec = pl.BlockSpec((1, d), lambda bi, i: (0, 0))
    return pl.pallas_call(
        functools.partial(_xattn_kernel, alpha=alpha),
        grid=(batch, seq // tm),
        in_specs=[xs, xs, ms, ms,
                  pl.BlockSpec((d, xw), lambda bi, i: (0, 0)),
                  pl.BlockSpec((xw, d), lambda bi, i: (0, 0)), vec, vec],
        out_specs=[xs, xs],
        out_shape=[jax.ShapeDtypeStruct((batch, seq, d), F32), jax.ShapeDtypeStruct((batch, seq, d), BF16)],
        compiler_params=_params("parallel", "parallel"),
    )(xb, xf, kmem, vmem, wq, wo, g.reshape(1, d), b.reshape(1, d))


def _ffn_kernel(be_ref, nu_ref, x_ref, w1_ref, w3_ref, w2_ref, g_ref, b_ref, *rest, alpha, final_ln):
    del be_ref
    if final_ln:
        of_ref, ob_ref, xb_scr, acc_scr = rest
    else:
        of_ref, xb_scr, acc_scr = rest
    f = pl.program_id(1)
    last = f == pl.num_programs(1) - 1
    active = pl.program_id(0) < nu_ref[0]

    @pl.when(active & (f == 0))
    def _():
        xb_scr[...] = x_ref[...].astype(BF16)
        acc_scr[...] = jnp.zeros_like(acc_scr)

    @pl.when(active)
    def _():
        xb = xb_scr[...]
        gate = jnp.dot(xb, w1_ref[0], preferred_element_type=F32)
        up = jnp.dot(xb, w3_ref[0], preferred_element_type=F32)
        hh = (gate * _sigmoid(gate) * up).astype(BF16)
        acc_scr[...] += jnp.dot(hh, w2_ref[0], preferred_element_type=F32)

    @pl.when(active & last)
    def _():
        if final_ln:
            y = _layer_norm(alpha * x_ref[...] + acc_scr[...], g_ref[...], b_ref[...], LN_EPS)
            of_ref[...] = y
            ob_ref[...] = y.astype(BF16)
        else:
            of_ref[...] = acc_scr[...]

    @pl.when(jnp.logical_not(active) & last)
    def _():
        of_ref[...] = jnp.zeros_like(of_ref)
        if final_ln:
            ob_ref[...] = jnp.zeros_like(ob_ref)


def swiglu(x, block_expert, n_used, w1, w3, w2, g, b, alpha, final_ln, tm=512, tf=512):
    rows, d = x.shape
    ff = w1.shape[2]
    tm = min(tm, rows)
    nf = ff // tf
    fi = lambda i, f, nu: jnp.where(i < nu[0], f, nf - 1)
    xs_in = pl.BlockSpec((tm, d), lambda i, f, be, nu: (jnp.minimum(i, nu[0] - 1), 0))
    xs = pl.BlockSpec((tm, d), lambda i, f, be, nu: (i, 0))
    vec = pl.BlockSpec((1, d), lambda i, f, be, nu: (0, 0))
    n_out = 2 if final_ln else 1
    out_shape = [jax.ShapeDtypeStruct((rows, d), F32), jax.ShapeDtypeStruct((rows, d), BF16)][:n_out]
    res = pl.pallas_call(
        functools.partial(_ffn_kernel, alpha=alpha, final_ln=final_ln),
        grid_spec=pltpu.PrefetchScalarGridSpec(
            num_scalar_prefetch=2,
            grid=(rows // tm, nf),
            in_specs=[xs_in,
                      pl.BlockSpec((1, d, tf), lambda i, f, be, nu: (be[i], 0, fi(i, f, nu))),
                      pl.BlockSpec((1, d, tf), lambda i, f, be, nu: (be[i], 0, fi(i, f, nu))),
                      pl.BlockSpec((1, tf, d), lambda i, f, be, nu: (be[i], fi(i, f, nu), 0)),
                      vec, vec],
            out_specs=[xs] * n_out,
            scratch_shapes=[pltpu.VMEM((tm, d), BF16), pltpu.VMEM((tm, d), F32)]),
        out_shape=out_shape,
        compiler_params=_params("parallel", "arbitrary"),
    )(block_expert, n_used, x, w1, w3, w2, g.reshape(1, d), b.reshape(1, d))
    return res if final_ln else res[0]


def _router_kernel(x_ref, w_ref, o_ref):
    x = x_ref[...]
    hi, mid, lo = _split3(x)
    whi, wmid, wlo = _split3(w_ref[...])
    dot = lambda a, b: jnp.dot(a, b, preferred_element_type=F32)
    logits = (dot(hi, whi) + (dot(hi, wmid) + dot(mid, whi))
              + (dot(mid, wmid) + dot(hi, wlo) + dot(lo, whi)))
    lane = lax.broadcasted_iota(jnp.int32, logits.shape, 1)
    neg = -jnp.inf
    lg = jnp.where(lane < N_EXPERTS, logits, neg)
    m1 = jnp.max(lg, axis=-1, keepdims=True)
    i1 = jnp.min(jnp.where(lg == m1, lane, LANES), axis=-1, keepdims=True)
    lg2 = jnp.where(lane == i1, neg, lg)
    m2 = jnp.max(lg2, axis=-1, keepdims=True)
    i2 = jnp.min(jnp.where(lg2 == m2, lane, LANES), axis=-1, keepdims=True)
    e = jnp.exp(m2 - m1)
    g1 = 1.0 / (1.0 + e)
    g2 = e / (1.0 + e)
    out = jnp.where(lane == 0, i1.astype(F32),
                    jnp.where(lane == 1, i2.astype(F32),
                              jnp.where(lane == 2, g1, jnp.where(lane == 3, g2, 0.0))))
    o_ref[...] = out


def route_top2(x, w_router_pad, tm=1024):
    t, d = x.shape
    tm = min(tm, t)
    return pl.pallas_call(
        _router_kernel,
        grid=(t // tm,),
        in_specs=[pl.BlockSpec((tm, d), lambda i: (i, 0)), pl.BlockSpec((d, LANES), lambda i: (0, 0))],
        out_specs=pl.BlockSpec((tm, LANES), lambda i: (i, 0)),
        out_shape=jax.ShapeDtypeStruct((t, LANES), F32),
        compiler_params=_params("parallel"),
    )(x, w_router_pad)


def _row_copy(src_hbm, dst_ref, sem, src_row, dst_row):
    return pltpu.make_async_copy(src_hbm.at[pl.ds(src_row, 1)], dst_ref.at[pl.ds(dst_row, 1)], sem)


def _gather_rows(idx_ref, src_hbm, dst_ref, sem, rows):
    def start(i, carry):
        for p in range(DMA_PRIORITIES):
            r = i * DMA_PRIORITIES + p
            _row_copy(src_hbm, dst_ref, sem, idx_ref[0, 0, r], r).start(priority=p)
        return carry

    def wait(r, carry):
        _row_copy(src_hbm, dst_ref, sem, 0, r).wait()
        return carry

    lax.fori_loop(0, rows // DMA_PRIORITIES, start, 0, unroll=DMA_UNROLL // DMA_PRIORITIES)
    lax.fori_loop(0, rows, wait, 0, unroll=DMA_UNROLL)


def _gather_rows_kernel(idx_ref, src_hbm, dst_ref, sem, *, rows):
    _gather_rows(idx_ref, src_hbm, dst_ref, sem, rows)


def gather_rows(src, idx, rows=512):
    n = idx.shape[0]
    d = src.shape[1]
    return pl.pallas_call(
        functools.partial(_gather_rows_kernel, rows=rows),
        grid=(n // rows,),
        in_specs=[pl.BlockSpec((1, 1, rows), lambda i: (i, 0, 0), memory_space=pltpu.SMEM),
                  pl.BlockSpec(memory_space=pl.ANY)],
        out_specs=pl.BlockSpec((rows, d), lambda i: (i, 0)),
        out_shape=jax.ShapeDtypeStruct((n, d), src.dtype),
        scratch_shapes=[pltpu.SemaphoreType.DMA(())],
        compiler_params=_params("arbitrary"),
    )(idx.reshape(n // rows, 1, rows), src)


def _combine_kernel(idx_ref, y_hbm, x_ref, gate_ref, g_ref, b_ref, of_ref, ob_ref, ybuf, sem, *, alpha):
    tm = x_ref.shape[0]
    _gather_rows(idx_ref, y_hbm, ybuf, sem, 2 * tm)
    gates = gate_ref[...]
    h = gates[:, 0:1] * ybuf[0:tm, :] + gates[:, 1:2] * ybuf[tm:2 * tm, :]
    y = _layer_norm(alpha * x_ref[...] + h, g_ref[...], b_ref[...], LN_EPS)
    of_ref[...] = y
    ob_ref[...] = y.astype(BF16)


def moe_combine_ln(y_slots, dest, gates, x, g, b, alpha, tm=256):
    t, d = x.shape
    tm = min(tm, t)
    idx = dest.reshape(t // tm, tm, 2).transpose(0, 2, 1).reshape(t // tm, 1, 2 * tm)
    row = lambda i: (i, 0)
    vec = pl.BlockSpec((1, d), lambda i: (0, 0))
    return pl.pallas_call(
        functools.partial(_combine_kernel, alpha=alpha),
        grid=(t // tm,),
        in_specs=[pl.BlockSpec((1, 1, 2 * tm), lambda i: (i, 0, 0), memory_space=pltpu.SMEM),
                  pl.BlockSpec(memory_space=pl.ANY),
                  pl.BlockSpec((tm, d), row), pl.BlockSpec((tm, 2), row), vec, vec],
        out_specs=[pl.BlockSpec((tm, d), row), pl.BlockSpec((tm, d), row)],
        out_shape=[jax.ShapeDtypeStruct((t, d), F32), jax.ShapeDtypeStruct((t, d), BF16)],
        scratch_shapes=[pltpu.VMEM((2 * tm, d), F32), pltpu.SemaphoreType.DMA(())],
        compiler_params=_params("arbitrary"),
    )(idx, y_slots, x, gates, g.reshape(1, d), b.reshape(1, d))


def moe_swiglu_ln(xf, w_router_pad, w1, w3, w2, g, b, alpha):
    t, d = xf.shape
    routed = route_top2(xf, w_router_pad)
    expert = routed[:, 0:2].astype(jnp.int32)
    gates = routed[:, 2:4]
    flat_e = expert.reshape(-1)
    onehot = (flat_e[:, None] == jnp.arange(N_EXPERTS, dtype=jnp.int32)[None, :]).astype(jnp.int32)
    rank = jnp.sum((jnp.cumsum(onehot, axis=0) - 1) * onehot, axis=1)
    counts = jnp.sum(onehot, axis=0)
    padded = (counts + MOE_ROWS - 1) // MOE_ROWS * MOE_ROWS
    ends = jnp.cumsum(padded)
    dest = (ends - padded)[flat_e] + rank
    n_blocks = (t * 2) // MOE_ROWS + N_EXPERTS
    n_slots = n_blocks * MOE_ROWS
    flat_tok = jnp.arange(t * 2, dtype=jnp.int32) // 2
    slot_tok = jnp.zeros((n_slots,), jnp.int32).at[dest].set(flat_tok)
    block_start = jnp.arange(n_blocks, dtype=jnp.int32) * MOE_ROWS
    block_expert = jnp.minimum(jnp.sum((ends[None, :] <= block_start[:, None]).astype(jnp.int32), axis=1),
                               N_EXPERTS - 1)
    x_slots = gather_rows(xf, slot_tok, rows=MOE_ROWS)
    n_used = (ends[N_EXPERTS - 1:] // MOE_ROWS).astype(jnp.int32)
    y_slots = swiglu(x_slots, block_expert, n_used, w1, w3, w2, g, b, alpha, final_ln=False, tm=MOE_ROWS)
    return moe_combine_ln(y_slots, dest.reshape(t, 2).astype(jnp.int32), gates, xf, g, b, alpha)


def kernel(x, mem, w_in, sg_w, sg_b, sg_ln_g, sg_ln_b, hg_lb_logits, hg_norm_g, rw_mu, rw_w0, rw_w2, rw_a0, rw_a2, rw_g2, rw_k_k, rw_k_a, rw_r_k, rw_gn_g, rw_gn_b, rw_v0, rw_v1, rw_v2, w_branch, w_mix_out, xa_wq, xa_wk, xa_wv, xa_wo, ln_g, ln_b, ffn_w1, ffn_w3, ffn_w2, moe_router, moe_w1, moe_w3, moe_w2):
    batch, seq, d = x.shape
    depth = w_in.shape[0]
    t = batch * seq
    alpha = (2 * depth) ** 0.25
    w = BRANCH_W
    sg_cols, hg_cols = 2 * w, 4 * w
    rw_cols = 3 * w + RW_W_LORA + RW_A_LORA + RW_G_LORA
    o_hg, o_rw, o_gate = sg_cols, sg_cols + hg_cols, sg_cols + hg_cols + rw_cols

    p = jax.nn.softmax(hg_lb_logits.astype(F32), axis=0)
    lower_bounds = jnp.cumsum(p, axis=0) - p[0]
    lane_head = jnp.arange(LANES) // RW_HEAD
    ones_blk = (lane_head[:, None] == lane_head[None, :]).astype(BF16)

    def pad_cols(m, n):
        return jnp.pad(m, ((0, 0), (0, n - m.shape[1])))

    def pad_rows(m, n):
        return jnp.pad(m, ((0, n - m.shape[0]), (0, 0)))

    mem_b = mem.reshape(-1, d).astype(BF16)
    xf = x.reshape(t, d)
    xb = xf.astype(BF16)
    v_first = None
    for l in range(depth):
        wl_ = w_in[l]
        c0 = o_rw + 3 * w
        w_rw = jnp.concatenate([wl_[:, o_rw:c0],
                                pad_cols(wl_[:, c0:c0 + RW_W_LORA], RW_LORA_PAD),
                                pad_cols(wl_[:, c0 + RW_W_LORA:c0 + RW_W_LORA + RW_A_LORA], RW_LORA_PAD),
                                wl_[:, c0 + RW_W_LORA + RW_A_LORA:o_gate]], axis=1)
        w_proj = jnp.concatenate([wl_[:, o_hg:o_rw], w_rw, wl_[:, :o_hg]], axis=1).astype(BF16)
        w_gate = wl_[:, o_gate:].reshape(d, 3, d).transpose(1, 0, 2).astype(BF16)
        mu = rw_mu[l]
        mu_p = jnp.concatenate([mu[:3 * w],
                                jnp.pad(mu[3 * w:3 * w + RW_W_LORA], (0, RW_LORA_PAD - RW_W_LORA)),
                                jnp.pad(mu[3 * w + RW_W_LORA:3 * w + RW_W_LORA + RW_A_LORA],
                                        (0, RW_LORA_PAD - RW_A_LORA)),
                                mu[3 * w + RW_W_LORA + RW_A_LORA:]])

        zall = matmul(xb, w_proj, F32, tn=1024)
        y_sg = sg_branch(zall, 4, sg_w[l], sg_b[l], sg_ln_g[l], sg_ln_b[l])
        y_hg = hgrn2_branch(zall.reshape(batch, seq, -1), 0, lower_bounds[l], hg_norm_g[l]).reshape(t, w)
        vmix = None if l == 0 else (rw_v0[l - 1], rw_v1[l - 1].astype(BF16), rw_v2[l - 1].astype(BF16))
        r_, lw_, k_, v_, kk_, al_, g_ = rw_prep(
            zall, 1, seq, mu_p, rw_w0[l], pad_rows(rw_w2[l], RW_LORA_PAD).astype(BF16), rw_a0[l],
            pad_rows(rw_a2[l], RW_LORA_PAD).astype(BF16), rw_g2[l].astype(BF16), rw_k_k[l], rw_k_a[l],
            ones_blk, vmix, v_first)
        if l == 0:
            v_first = v_
        y_rw = rwkv7_mix(r_, lw_, k_, v_, kk_, al_, g_, rw_gn_g[l], rw_gn_b[l], rw_r_k[l], ones_blk,
                         batch, seq).reshape(t, w)
        merged = merge_branches(xb, y_sg, y_hg, y_rw, w_gate, w_branch[l].astype(BF16))
        xf, xb = matmul_res_ln(merged, w_mix_out[l].astype(BF16), xf, ln_g[l, 0], ln_b[l, 0], alpha)

        kmem = matmul(mem_b, xa_wk[l].astype(BF16), BF16).reshape(batch, -1, XA_HEADS * XA_HEAD_DIM)
        vmem = matmul(mem_b, xa_wv[l].astype(BF16), BF16).reshape(batch, -1, XA_HEADS * XA_HEAD_DIM)
        xf3, xb3 = cross_attention_ln(xb.reshape(batch, seq, d), xf.reshape(batch, seq, d), kmem, vmem,
                                      xa_wq[l].astype(BF16), xa_wo[l].astype(BF16), ln_g[l, 1], ln_b[l, 1], alpha)
        xf, xb = xf3.reshape(t, d), xb3.reshape(t, d)

        if l % 2 == 0:
            i = l // 2
            n_row_blocks = t // min(512, t)
            xf, xb = swiglu(xf, jnp.zeros((n_row_blocks,), jnp.int32), jnp.full((1,), n_row_blocks, jnp.int32),
                            ffn_w1[i:i + 1].astype(BF16),
                            ffn_w3[i:i + 1].astype(BF16), ffn_w2[i:i + 1].astype(BF16),
                            ln_g[l, 2], ln_b[l, 2], alpha, final_ln=True)
        else:
            i = l // 2
            xf, xb = moe_swiglu_ln(xf, pad_cols(moe_router[i], LANES), moe_w1[i].astype(BF16),
                                   moe_w3[i].astype(BF16), moe_w2[i].astype(BF16), ln_g[l, 2], ln_b[l, 2], alpha)
    return xf.reshape(batch, seq, d)
```

```python
import functools
import math

import jax
import jax.numpy as jnp
from jax import lax
from jax.experimental import pallas as pl
from jax.experimental.pallas import tpu as pltpu

F32 = jnp.float32
BF16 = jnp.bfloat16

V7X_VMEM_BYTES = 64 * 1024 * 1024
VMEM_LIMIT = V7X_VMEM_BYTES - 8 * 1024 * 1024
LANES = 128
SUBLANES = 8

BRANCH_W = 512
SG_CHUNK = 128
SG_GROUPS = 4
HG_HEADS = 4
HG_DK = BRANCH_W // HG_HEADS
HG_CHUNK = 16
HG_EPS = 1e-6
HG_EXP_CLIP = 60.0
RW_HEAD = 64
RW_HEADS = BRANCH_W // RW_HEAD
RW_CHUNK = 64
RW_W_LORA = 96
RW_A_LORA = 96
RW_G_LORA = 256
RW_GN_EPS = 64e-5
RW_LORA_PAD = 128
XA_HEADS = 4
XA_HEAD_DIM = 128
N_EXPERTS = 8
MOE_ROWS = 512
LN_EPS = 1e-5
DMA_UNROLL = 8
DMA_PRIORITIES = 2


def _params(*sem):
    return pltpu.CompilerParams(dimension_semantics=sem, vmem_limit_bytes=VMEM_LIMIT)


def _dot(a, b):
    return jnp.dot(a.astype(BF16), b.astype(BF16), preferred_element_type=F32)


def _dot_nt(a, b):
    return lax.dot_general(a.astype(BF16), b.astype(BF16), (((1,), (1,)), ((), ())),
                           preferred_element_type=F32)


def _dot_tn(a, b):
    return lax.dot_general(a.astype(BF16), b.astype(BF16), (((0,), (0,)), ((), ())),
                           preferred_element_type=F32)


def _split3(x):
    hi = x.astype(BF16)
    r1 = x - hi.astype(F32)
    mid = r1.astype(BF16)
    lo = (r1 - mid.astype(F32)).astype(BF16)
    return hi, mid, lo


def _dot_exact_lhs(m_bf16, x):
    hi, mid, lo = _split3(x)
    return (jnp.dot(m_bf16, hi, preferred_element_type=F32)
            + jnp.dot(m_bf16, mid, preferred_element_type=F32)
            + jnp.dot(m_bf16, lo, preferred_element_type=F32))


def _dot_exact_rhs(x, m_bf16):
    hi, mid, lo = _split3(x)
    return (jnp.dot(hi, m_bf16, preferred_element_type=F32)
            + jnp.dot(mid, m_bf16, preferred_element_type=F32)
            + jnp.dot(lo, m_bf16, preferred_element_type=F32))


def _head_sum(x, ones_blk):
    n = x.shape[-1] // LANES
    return jnp.concatenate(
        [_dot_exact_rhs(x[:, i * LANES:(i + 1) * LANES], ones_blk) for i in range(n)], axis=-1)


def _sigmoid(x):
    return 1.0 / (1.0 + jnp.exp(-x))


def _log_sigmoid(x):
    return jnp.minimum(x, 0.0) - jnp.log1p(jnp.exp(-jnp.abs(x)))


def _layer_norm(x, g, b, eps):
    mu = jnp.mean(x, axis=-1, keepdims=True)
    xc = x - mu
    var = jnp.mean(xc * xc, axis=-1, keepdims=True)
    return xc * lax.rsqrt(var + eps) * g + b


def _tril_mask(n, strict=False):
    row = lax.broadcasted_iota(jnp.int32, (n, n), 0)
    col = lax.broadcasted_iota(jnp.int32, (n, n), 1)
    return (row > col) if strict else (row >= col)


def _mm_kernel(a_ref, b_ref, o_ref):
    o_ref[...] = jnp.dot(a_ref[...], b_ref[...], preferred_element_type=F32).astype(o_ref.dtype)


def matmul(a, b, out_dtype, tm=1024, tn=512):
    m, k = a.shape
    n = b.shape[1]
    tm, tn = min(tm, m), min(tn, n)
    return pl.pallas_call(
        _mm_kernel,
        grid=(m // tm, n // tn),
        in_specs=[pl.BlockSpec((tm, k), lambda i, j: (i, 0)),
                  pl.BlockSpec((k, tn), lambda i, j: (0, j))],
        out_specs=pl.BlockSpec((tm, tn), lambda i, j: (i, j)),
        out_shape=jax.ShapeDtypeStruct((m, n), out_dtype),
        compiler_params=_params("parallel", "parallel"),
    )(a, b)


def _mm_res_ln_kernel(a_ref, w_ref, res_ref, g_ref, b_ref, of_ref, ob_ref, *, alpha):
    h = jnp.dot(a_ref[...], w_ref[...], preferred_element_type=F32)
    y = _layer_norm(alpha * res_ref[...] + h, g_ref[...], b_ref[...], LN_EPS)
    of_ref[...] = y
    ob_ref[...] = y.astype(BF16)


def matmul_res_ln(a, w, res, g, b, alpha, tm=512):
    m, k = a.shape
    n = w.shape[1]
    tm = min(tm, m)
    return pl.pallas_call(
        functools.partial(_mm_res_ln_kernel, alpha=alpha),
        grid=(m // tm,),
        in_specs=[pl.BlockSpec((tm, k), lambda i: (i, 0)),
                  pl.BlockSpec((k, n), lambda i: (0, 0)),
                  pl.BlockSpec((tm, n), lambda i: (i, 0)),
                  pl.BlockSpec((1, n), lambda i: (0, 0)),
                  pl.BlockSpec((1, n), lambda i: (0, 0))],
        out_specs=[pl.BlockSpec((tm, n), lambda i: (i, 0)),
                   pl.BlockSpec((tm, n), lambda i: (i, 0))],
        out_shape=[jax.ShapeDtypeStruct((m, n), F32), jax.ShapeDtypeStruct((m, n), BF16)],
        compiler_params=_params("parallel"),
    )(a, w, res, g.reshape(1, n), b.reshape(1, n))


def _sg_kernel(z_ref, w_ref, b_ref, g_ref, beta_ref, y_ref, *, n_chunks):
    z = z_ref[...]
    z = 0.5 * z * (1.0 + jnp.tanh(math.sqrt(2.0 / math.pi) * (z + 0.044715 * (z * z * z))))
    u = z[:, :BRANCH_W]
    v = _layer_norm(z[:, BRANCH_W:], g_ref[...], beta_ref[...], LN_EPS).astype(BF16)
    causal = _tril_mask(SG_CHUNK)
    gw = BRANCH_W // SG_GROUPS
    for g in range(SG_GROUPS):
        wg = jnp.where(causal, w_ref[g], 0.0).astype(BF16)
        for c in range(n_chunks):
            rows = slice(c * SG_CHUNK, (c + 1) * SG_CHUNK)
            cols = slice(g * gw, (g + 1) * gw)
            s = jnp.dot(wg, v[rows, cols], preferred_element_type=F32) + b_ref[g]
            y_ref[rows, cols] = (u[rows, cols] * s).astype(BF16)


def sg_branch(zall, col_block, sg_w, sg_b, ln_g, ln_b, tm=512):
    t = zall.shape[0]
    tm = min(tm, t)
    return pl.pallas_call(
        functools.partial(_sg_kernel, n_chunks=tm // SG_CHUNK),
        grid=(t // tm,),
        in_specs=[pl.BlockSpec((tm, 2 * BRANCH_W), lambda i: (i, col_block)),
                  pl.BlockSpec((SG_GROUPS, SG_CHUNK, SG_CHUNK), lambda i: (0, 0, 0)),
                  pl.BlockSpec((SG_GROUPS, SG_CHUNK, 1), lambda i: (0, 0, 0)),
                  pl.BlockSpec((1, BRANCH_W), lambda i: (0, 0)),
                  pl.BlockSpec((1, BRANCH_W), lambda i: (0, 0))],
        out_specs=pl.BlockSpec((tm, BRANCH_W), lambda i: (i, 0)),
        out_shape=jax.ShapeDtypeStruct((t, BRANCH_W), BF16),
        compiler_params=_params("parallel"),
    )(zall, sg_w, sg_b.reshape(SG_GROUPS, SG_CHUNK, 1), ln_g.reshape(1, -1), ln_b.reshape(1, -1))


def _hgrn_kernel(z_ref, lb_ref, ng_ref, y_ref, st_ref, *, batch):
    c = HG_CHUNK

    @pl.when(pl.program_id(0) == 0)
    def _():
        st_ref[...] = jnp.zeros_like(st_ref)

    lb = lb_ref[...]
    ltri = jnp.where(_tril_mask(c), 1.0, 0.0).astype(BF16)
    rows = lax.broadcasted_iota(jnp.int32, (c, 1), 0)
    w = BRANCH_W
    for b in range(batch):
        zq = z_ref[b, :, 0:w]
        zf = z_ref[b, :, w:2 * w]
        zi = z_ref[b, :, 2 * w:3 * w]
        zg = z_ref[b, :, 3 * w:4 * w]
        log_f = _log_sigmoid(zf) + jnp.log1p(lb * jnp.exp(jnp.minimum(-zf, HG_EXP_CLIP)))
        log_f = jnp.minimum(log_f, 0.0)
        kx = (1.0 - lb) * _sigmoid(-zf)
        q = zq * _sigmoid(zq)
        bc = _dot_exact_lhs(ltri, log_f)
        b_last = bc[c - 1:c, :]
        qd = q * jnp.exp(bc)
        kd = kx * jnp.exp(b_last - bc)
        e_last = jnp.exp(b_last)
        outs = []
        for h in range(HG_HEADS):
            sl = slice(h * HG_DK, (h + 1) * HG_DK)
            st = st_ref[b * HG_HEADS + h]
            o = _dot_nt(qd[:, sl], st)
            qh, kh, bh, vh = q[:, sl], kx[:, sl], bc[:, sl], zi[:, sl]
            parts = []
            for r0 in range(0, c, SUBLANES):
                rs = slice(r0, r0 + SUBLANES)
                qg, bg, og, rg = qh[rs], bh[rs], o[rs], rows[rs]
                for s in range(r0 + SUBLANES):
                    m = qg * kh[s:s + 1, :] * jnp.exp(jnp.minimum(bg - bh[s:s + 1, :], 0.0))
                    att = jnp.sum(m, axis=-1, keepdims=True)
                    if s > r0:
                        att = jnp.where(rg >= s, att, 0.0)
                    og = og + att * vh[s:s + 1, :]
                parts.append(og)
            o = jnp.concatenate(parts, axis=0)
            st_ref[b * HG_HEADS + h] = st * e_last[:, sl] + _dot_tn(vh, kd[:, sl])
            outs.append(o)
        o = jnp.concatenate(outs, axis=-1) * _sigmoid(zg)
        y = o * lax.rsqrt(jnp.mean(o * o, axis=-1, keepdims=True) + HG_EPS) * ng_ref[...]
        y_ref[b] = y.astype(BF16)


def hgrn2_branch(z3, col_block, lb, norm_g):
    batch, seq, _ = z3.shape
    return pl.pallas_call(
        functools.partial(_hgrn_kernel, batch=batch),
        grid=(seq // HG_CHUNK,),
        in_specs=[pl.BlockSpec((batch, HG_CHUNK, 4 * BRANCH_W), lambda c: (0, c, col_block)),
                  pl.BlockSpec((1, BRANCH_W), lambda c: (0, 0)),
                  pl.BlockSpec((1, BRANCH_W), lambda c: (0, 0))],
        out_specs=pl.BlockSpec((batch, HG_CHUNK, BRANCH_W), lambda c: (0, c, 0)),
        out_shape=jax.ShapeDtypeStruct((batch, seq, BRANCH_W), BF16),
        scratch_shapes=[pltpu.VMEM((batch * HG_HEADS, HG_DK, HG_DK), F32)],
        compiler_params=_params("arbitrary"),
    )(z3, lb.reshape(1, -1), norm_g.reshape(1, -1))


def _rw_prep_kernel(*refs, tiles_per_seq, has_vmix):
    if has_vmix:
        (z_ref, zp_ref, mu_ref, w0_ref, w2_ref, a0_ref, a2_ref, g2_ref, kk_ref, ka_ref, ones_ref,
         v0_ref, v1_ref, v2_ref, vf_ref, r_o, lw_o, k_o, v_o, kk_o, al_o, g_o) = refs
    else:
        (z_ref, zp_ref, mu_ref, w0_ref, w2_ref, a0_ref, a2_ref, g2_ref, kk_ref, ka_ref, ones_ref,
         r_o, lw_o, k_o, v_o, kk_o, al_o, g_o) = refs
    z = z_ref[...]
    tm = z.shape[0]
    first = (pl.program_id(0) % tiles_per_seq) == 0
    prev = jnp.where(first, 0.0, zp_ref[7:8, :])
    rows = lax.broadcasted_iota(jnp.int32, (tm, 1), 0)
    zs = jnp.where(rows == 0, prev, pltpu.roll(z, 1, 0))
    zc = z + (zs - z) * mu_ref[...]
    w = BRANCH_W
    p = RW_LORA_PAD
    r, k, v = zc[:, 0:w], zc[:, w:2 * w], zc[:, 2 * w:3 * w]
    wl, al_in, gl = zc[:, 3 * w:3 * w + p], zc[:, 3 * w + p:3 * w + 2 * p], zc[:, 3 * w + 2 * p:]
    u = w0_ref[...] + _dot(jnp.tanh(wl), w2_ref[...])
    lw = -math.exp(-0.5) * _sigmoid(u)
    a = _sigmoid(a0_ref[...] + _dot(al_in, a2_ref[...]))
    g = _dot(_sigmoid(gl), g2_ref[...])
    if has_vmix:
        mix = _sigmoid(v0_ref[...] + _dot(_dot(v, v1_ref[...]), v2_ref[...]))
        v = v + (vf_ref[...] - v) * mix
    kkx = k * kk_ref[...]
    norm = jnp.sqrt(_head_sum(kkx * kkx, ones_ref[...]))
    kkn = kkx / jnp.maximum(norm, 1e-12)
    r_o[...] = r
    lw_o[...] = lw
    k_o[...] = k * (1.0 + (a - 1.0) * ka_ref[...])
    v_o[...] = v
    kk_o[...] = kkn
    al_o[...] = a
    g_o[...] = g


def rw_prep(zall, col_block, seq, mu, w0, w2p, a0, a2p, g2, k_k, k_a, ones_blk, vmix, v_first, tm=256):
    t = zall.shape[0]
    tm = min(tm, seq)
    zw = 4 * BRANCH_W
    row = lambda i: (i, 0)
    const = lambda i: (0, 0)
    vec = pl.BlockSpec((1, BRANCH_W), const)
    in_specs = [pl.BlockSpec((tm, zw), lambda i: (i, col_block)),
                pl.BlockSpec((8, zw), lambda i: (jnp.maximum(i * (tm // 8) - 1, 0), col_block)),
                pl.BlockSpec((1, zw), const), vec,
                pl.BlockSpec((RW_LORA_PAD, BRANCH_W), const), vec,
                pl.BlockSpec((RW_LORA_PAD, BRANCH_W), const),
                pl.BlockSpec((RW_G_LORA, BRANCH_W), const), vec, vec,
                pl.BlockSpec((LANES, LANES), const)]
    args = [zall, zall, mu.reshape(1, -1), w0.reshape(1, -1), w2p, a0.reshape(1, -1), a2p, g2,
            k_k.reshape(1, -1), k_a.reshape(1, -1), ones_blk]
    if vmix is not None:
        v0, v1, v2 = vmix
        in_specs += [vec, pl.BlockSpec(v1.shape, const), pl.BlockSpec(v2.shape, const),
                     pl.BlockSpec((tm, BRANCH_W), row)]
        args += [v0.reshape(1, -1), v1, v2, v_first]
    out = jax.ShapeDtypeStruct((t, BRANCH_W), F32)
    return pl.pallas_call(
        functools.partial(_rw_prep_kernel, tiles_per_seq=seq // tm, has_vmix=vmix is not None),
        grid=(t // tm,),
        in_specs=in_specs,
        out_specs=[pl.BlockSpec((tm, BRANCH_W), row)] * 7,
        out_shape=[out] * 7,
        compiler_params=_params("parallel"),
    )(*args)


def _rwkv_kernel(r_ref, lw_ref, k_ref, v_ref, kk_ref, al_ref, g_ref, gng_ref, gnb_ref, rk_ref, ones_ref,
                 y_ref, st_ref, *, batch):
    c = RW_CHUNK

    @pl.when(pl.program_id(0) == 0)
    def _():
        st_ref[...] = jnp.zeros_like(st_ref)

    incl = _tril_mask(c)
    strict = _tril_mask(c, strict=True)
    ltri = jnp.where(incl, 1.0, 0.0).astype(BF16)
    units = [(b, h) for b in range(batch) for h in range(RW_HEADS)]
    ar, bk, vv, st, g_last = {}, {}, {}, {}, {}
    for b in range(batch):
        lw, kk = lw_ref[b], kk_ref[b]
        cum = _dot_exact_lhs(ltri, lw)
        gam = jnp.exp(cum)
        ginv = jnp.exp(-cum)
        at = -kk * jnp.exp(cum - lw)
        rt = r_ref[b] * gam
        bt = kk * al_ref[b] * ginv
        kt = k_ref[b] * ginv
        v = v_ref[b]
        for h in range(RW_HEADS):
            sl = slice(h * RW_HEAD, (h + 1) * RW_HEAD)
            ar[b, h] = jnp.concatenate([at[:, sl], rt[:, sl]], axis=0).astype(BF16)
            bk[b, h] = jnp.concatenate([bt[:, sl], kt[:, sl]], axis=0).astype(BF16)
            vv[b, h] = v[:, sl]
            st[b, h] = st_ref[b * RW_HEADS + h]
            g_last[b, h] = gam[c - 1:c, sl]
    nb = {u: _dot_nt(ar[u], bk[u][:c]) for u in units}
    nk = {u: _dot_nt(ar[u], bk[u][c:]) for u in units}
    a_s = {u: _dot_nt(ar[u], st[u]) for u in units}
    n = {u: jnp.where(strict, nb[u][:c], 0.0).astype(BF16) for u in units}
    rb = {u: jnp.where(incl, nb[u][c:], 0.0) for u in units}
    akrk = {u: jnp.concatenate([jnp.where(strict, nk[u][:c], 0.0), jnp.where(incl, nk[u][c:], 0.0)], axis=0)
            for u in units}
    kv = {u: _dot(akrk[u], vv[u]) for u in units}
    x = {u: a_s[u][:c] + kv[u][:c] for u in units}
    uu = {u: x[u] + _dot(n[u], x[u]) for u in units}
    p = n
    for _ in range(int(math.log2(c)) - 1):
        p = {u: _dot(p[u], p[u]).astype(BF16) for u in units}
        uu = {u: uu[u] + _dot(p[u], uu[u]) for u in units}
    y = {u: a_s[u][c:] + kv[u][c:] + _dot(rb[u], uu[u]) for u in units}
    for u in units:
        upd = _dot_tn(jnp.concatenate([uu[u], vv[u]], axis=0), bk[u])
        st_ref[u[0] * RW_HEADS + u[1]] = (st[u] + upd) * g_last[u]
    ones = ones_ref[...]
    inv_n = 1.0 / RW_HEAD
    for b in range(batch):
        yb = jnp.concatenate([y[b, h] for h in range(RW_HEADS)], axis=-1)
        mu_y = _head_sum(yb, ones) * inv_n
        yc = yb - mu_y
        var = _head_sum(yc * yc, ones) * inv_n
        yn = yc * lax.rsqrt(var + RW_GN_EPS) * gng_ref[...] + gnb_ref[...]
        bonus = _head_sum(r_ref[b] * k_ref[b] * rk_ref[...], ones)
        y_ref[b] = ((yn + bonus * v_ref[b]) * g_ref[b]).astype(BF16)


def rwkv7_mix(r, lw, k, v, kk, al, g, gn_g, gn_b, r_k, ones_blk, batch, seq):
    arrs = [x.reshape(batch, seq, BRANCH_W) for x in (r, lw, k, v, kk, al, g)]
    blk = pl.BlockSpec((batch, RW_CHUNK, BRANCH_W), lambda c: (0, c, 0))
    vec = pl.BlockSpec((1, BRANCH_W), lambda c: (0, 0))
    return pl.pallas_call(
        functools.partial(_rwkv_kernel, batch=batch),
        grid=(seq // RW_CHUNK,),
        in_specs=[blk] * 7 + [vec] * 3 + [pl.BlockSpec((LANES, LANES), lambda c: (0, 0))],
        out_specs=blk,
        out_shape=jax.ShapeDtypeStruct((batch, seq, BRANCH_W), BF16),
        scratch_shapes=[pltpu.VMEM((batch * RW_HEADS, RW_HEAD, RW_HEAD), F32)],
        compiler_params=_params("arbitrary"),
    )(*arrs, gn_g.reshape(1, -1), gn_b.reshape(1, -1), r_k.reshape(1, -1), ones_blk)


def _merge_kernel(x_ref, ysg_ref, yhg_ref, yrw_ref, wg_ref, wb_ref, o_ref):
    x = x_ref[...]
    acc = None
    for b, y_ref in enumerate((ysg_ref, yhg_ref, yrw_ref)):
        gate = _sigmoid(jnp.dot(x, wg_ref[b], preferred_element_type=F32))
        term = gate * jnp.dot(y_ref[...], wb_ref[b], preferred_element_type=F32)
        acc = term if acc is None else acc + term
    o_ref[...] = acc.astype(BF16)


def merge_branches(xb, y_sg, y_hg, y_rw, w_gate, w_branch, tm=1024, tn=512):
    t, d = xb.shape
    tm = min(tm, t)
    ysp = pl.BlockSpec((tm, BRANCH_W), lambda i, j: (i, 0))
    return pl.pallas_call(
        _merge_kernel,
        grid=(t // tm, d // tn),
        in_specs=[pl.BlockSpec((tm, d), lambda i, j: (i, 0)), ysp, ysp, ysp,
                  pl.BlockSpec((3, d, tn), lambda i, j: (0, 0, j)),
                  pl.BlockSpec((3, BRANCH_W, tn), lambda i, j: (0, 0, j))],
        out_specs=pl.BlockSpec((tm, tn), lambda i, j: (i, j)),
        out_shape=jax.ShapeDtypeStruct((t, d), BF16),
        compiler_params=_params("parallel", "parallel"),
    )(xb, y_sg, y_hg, y_rw, w_gate, w_branch)


def _xattn_kernel(xb_ref, xf_ref, k_ref, v_ref, wq_ref, wo_ref, g_ref, b_ref, of_ref, ob_ref, *, alpha):
    q = jnp.dot(xb_ref[0], wq_ref[...], preferred_element_type=F32)
    kmem, vmem = k_ref[0], v_ref[0]
    scale = XA_HEAD_DIM ** -0.5
    outs = []
    for h in range(XA_HEADS):
        sl = slice(h * XA_HEAD_DIM, (h + 1) * XA_HEAD_DIM)
        s = _dot_nt(q[:, sl], kmem[:, sl]) * scale
        e = jnp.exp(s - jnp.max(s, axis=-1, keepdims=True))
        p = e / jnp.sum(e, axis=-1, keepdims=True)
        outs.append(_dot(p, vmem[:, sl]))
    o = jnp.concatenate(outs, axis=-1)
    h_out = _dot(o, wo_ref[...])
    y = _layer_norm(alpha * xf_ref[0] + h_out, g_ref[...], b_ref[...], LN_EPS)
    of_ref[0] = y
    ob_ref[0] = y.astype(BF16)


def cross_attention_ln(xb, xf, kmem, vmem, wq, wo, g, b, alpha, tm=512):
    batch, seq, d = xf.shape
    m, xw = kmem.shape[1], kmem.shape[2]
    tm = min(tm, seq)
    xs = pl.BlockSpec((1, tm, d), lambda bi, i: (bi, i, 0))
    ms = pl.BlockSpec((1, m, xw), lambda bi, i: (bi, 0, 0))
    vec = pl.BlockSpec((1, d), lambda bi, i: (0, 0))
    return pl.pallas_call(
        functools.partial(_xattn_kernel, alpha=alpha),
        grid=(batch, seq // tm),
        in_specs=[xs, xs, ms, ms,
                  pl.BlockSpec((d, xw), lambda bi, i: (0, 0)),
                  pl.BlockSpec((xw, d), lambda bi, i: (0, 0)), vec, vec],
        out_specs=[xs, xs],
        out_shape=[jax.ShapeDtypeStruct((batch, seq, d), F32), jax.ShapeDtypeStruct((batch, seq, d), BF16)],
        compiler_params=_params("parallel", "parallel"),
    )(xb, xf, kmem, vmem, wq, wo, g.reshape(1, d), b.reshape(1, d))


def _ffn_kernel(be_ref, nu_ref, *refs, alpha, final_ln, gather, nf):
    del be_ref
    if gather:
        idx0_ref, idxn_ref, x_ref, w1_ref, w3_ref, w2_ref, g_ref, b_ref, of_ref, xb_scr, acc_scr, xg_scr, sems = refs
    elif final_ln:
        x_ref, w1_ref, w3_ref, w2_ref, g_ref, b_ref, of_ref, ob_ref, xb_scr, acc_scr = refs
    else:
        x_ref, w1_ref, w3_ref, w2_ref, g_ref, b_ref, of_ref, xb_scr, acc_scr = refs
    i = pl.program_id(0)
    f = pl.program_id(1)
    last = f == nf - 1
    n_used = nu_ref[0]
    active = i < n_used
    tm = xb_scr.shape[0]
    per_step = -(-tm // nf)
    slot = i % 2

    if gather:
        @pl.when((i == 0) & (f == 0))
        def _():
            _gather_rows(idx0_ref, x_ref, xg_scr.at[0], sems.at[0], tm)

        @pl.when((f == 0) & (i >= 1) & (i <= n_used))
        def _():
            def wait(r, carry):
                _row_copy(x_ref, xg_scr.at[slot], sems.at[slot], 0, r).wait()
                return carry
            lax.fori_loop(0, per_step * nf, wait, 0, unroll=DMA_UNROLL)

    @pl.when(active & (f == 0))
    def _():
        if gather:
            xb_scr[...] = xg_scr[slot, 0:tm, :].astype(BF16)
        else:
            xb_scr[...] = x_ref[...].astype(BF16)
        acc_scr[...] = jnp.zeros_like(acc_scr)

    @pl.when(active)
    def _():
        if gather:
            for j in range(per_step):
                r = f * per_step + j
                src_row = idxn_ref[0, 0, jnp.minimum(r, tm - 1)]
                _row_copy(x_ref, xg_scr.at[1 - slot], sems.at[1 - slot], src_row, r).start(priority=1)
        xb = xb_scr[...]
        gate = jnp.dot(xb, w1_ref[0], preferred_element_type=F32)
        up = jnp.dot(xb, w3_ref[0], preferred_element_type=F32)
        hh = (gate * _sigmoid(gate) * up).astype(BF16)
        acc_scr[...] += jnp.dot(hh, w2_ref[0], preferred_element_type=F32)

    @pl.when(active & last)
    def _():
        if final_ln:
            y = _layer_norm(alpha * x_ref[...] + acc_scr[...], g_ref[...], b_ref[...], LN_EPS)
            of_ref[...] = y
            ob_ref[...] = y.astype(BF16)
        else:
            of_ref[...] = acc_scr[...]

    @pl.when(jnp.logical_not(active) & last)
    def _():
        of_ref[...] = jnp.zeros_like(of_ref)
        if final_ln:
            ob_ref[...] = jnp.zeros_like(ob_ref)


def swiglu(x, block_expert, n_used, w1, w3, w2, g, b, alpha, final_ln, slot_rows=None, tm=512, tf=512):
    gather = slot_rows is not None
    assert not (gather and final_ln)
    d = x.shape[1]
    rows = slot_rows.shape[0] if gather else x.shape[0]
    ff = w1.shape[2]
    tm = min(tm, rows)
    nf = ff // tf
    n_blocks = rows // tm
    fi = lambda i, f, nu: jnp.where(i < nu[0], f, nf - 1)
    xs = pl.BlockSpec((tm, d), lambda i, f, be, nu: (i, 0))
    vec = pl.BlockSpec((1, d), lambda i, f, be, nu: (0, 0))
    weights = [pl.BlockSpec((1, d, tf), lambda i, f, be, nu: (be[i], 0, fi(i, f, nu))),
               pl.BlockSpec((1, d, tf), lambda i, f, be, nu: (be[i], 0, fi(i, f, nu))),
               pl.BlockSpec((1, tf, d), lambda i, f, be, nu: (be[i], fi(i, f, nu), 0)),
               vec, vec]
    scratch = [pltpu.VMEM((tm, d), BF16), pltpu.VMEM((tm, d), F32)]
    if gather:
        idx = slot_rows.reshape(n_blocks, 1, tm)
        smem_rows = lambda index_map: pl.BlockSpec((1, 1, tm), index_map, memory_space=pltpu.SMEM)
        in_specs = [smem_rows(lambda i, f, be, nu: (0, 0, 0)),
                    smem_rows(lambda i, f, be, nu: (jnp.minimum(i + 1, n_blocks - 1), 0, 0)),
                    pl.BlockSpec(memory_space=pl.ANY)] + weights
        args = (idx, idx, x)
        fetch_rows = -(-tm // nf) * nf
        scratch += [pltpu.VMEM((2, -(-fetch_rows // SUBLANES) * SUBLANES, d), F32), pltpu.SemaphoreType.DMA((2,))]
    else:
        in_specs = [pl.BlockSpec((tm, d), lambda i, f, be, nu: (jnp.minimum(i, nu[0] - 1), 0))] + weights
        args = (x,)
    n_out = 2 if final_ln else 1
    out_shape = [jax.ShapeDtypeStruct((rows, d), F32), jax.ShapeDtypeStruct((rows, d), BF16)][:n_out]
    res = pl.pallas_call(
        functools.partial(_ffn_kernel, alpha=alpha, final_ln=final_ln, gather=gather, nf=nf),
        grid_spec=pltpu.PrefetchScalarGridSpec(
            num_scalar_prefetch=2,
            grid=(n_blocks, nf),
            in_specs=in_specs,
            out_specs=[xs] * n_out,
            scratch_shapes=scratch),
        out_shape=out_shape,
        compiler_params=_params("arbitrary" if gather else "parallel", "arbitrary"),
    )(block_expert, n_used, *args, w1, w3, w2, g.reshape(1, d), b.reshape(1, d))
    return res if final_ln else res[0]


def _router_kernel(x_ref, w_ref, o_ref):
    x = x_ref[...]
    hi, mid, lo = _split3(x)
    whi, wmid, wlo = _split3(w_ref[...])
    dot = lambda a, b: jnp.dot(a, b, preferred_element_type=F32)
    logits = (dot(hi, whi) + (dot(hi, wmid) + dot(mid, whi))
              + (dot(mid, wmid) + dot(hi, wlo) + dot(lo, whi)))
    lane = lax.broadcasted_iota(jnp.int32, logits.shape, 1)
    neg = -jnp.inf
    lg = jnp.where(lane < N_EXPERTS, logits, neg)
    m1 = jnp.max(lg, axis=-1, keepdims=True)
    i1 = jnp.min(jnp.where(lg == m1, lane, LANES), axis=-1, keepdims=True)
    lg2 = jnp.where(lane == i1, neg, lg)
    m2 = jnp.max(lg2, axis=-1, keepdims=True)
    i2 = jnp.min(jnp.where(lg2 == m2, lane, LANES), axis=-1, keepdims=True)
    e = jnp.exp(m2 - m1)
    g1 = 1.0 / (1.0 + e)
    g2 = e / (1.0 + e)
    out = jnp.where(lane == 0, i1.astype(F32),
                    jnp.where(lane == 1, i2.astype(F32),
                              jnp.where(lane == 2, g1, jnp.where(lane == 3, g2, 0.0))))
    o_ref[...] = out


def route_top2(x, w_router_pad, tm=1024):
    t, d = x.shape
    tm = min(tm, t)
    return pl.pallas_call(
        _router_kernel,
        grid=(t // tm,),
        in_specs=[pl.BlockSpec((tm, d), lambda i: (i, 0)), pl.BlockSpec((d, LANES), lambda i: (0, 0))],
        out_specs=pl.BlockSpec((tm, LANES), lambda i: (i, 0)),
        out_shape=jax.ShapeDtypeStruct((t, LANES), F32),
        compiler_params=_params("parallel"),
    )(x, w_router_pad)


def _row_copy(src_hbm, dst_ref, sem, src_row, dst_row):
    return pltpu.make_async_copy(src_hbm.at[pl.ds(src_row, 1)], dst_ref.at[pl.ds(dst_row, 1)], sem)


def _gather_rows(idx_ref, src_hbm, dst_ref, sem, rows):
    def start(i, carry):
        for p in range(DMA_PRIORITIES):
            r = i * DMA_PRIORITIES + p
            _row_copy(src_hbm, dst_ref, sem, idx_ref[0, 0, r], r).start(priority=p)
        return carry

    def wait(r, carry):
        _row_copy(src_hbm, dst_ref, sem, 0, r).wait()
        return carry

    lax.fori_loop(0, rows // DMA_PRIORITIES, start, 0, unroll=DMA_UNROLL // DMA_PRIORITIES)
    lax.fori_loop(0, rows, wait, 0, unroll=DMA_UNROLL)


def _combine_kernel(idx0_ref, idxn_ref, y_hbm, x_ref, gate_ref, g_ref, b_ref, of_ref, ob_ref, ybuf, sems, *, alpha):
    tm = x_ref.shape[0]
    i = pl.program_id(0)
    slot = i % 2

    def wait_rows(s):
        def wait(r, carry):
            _row_copy(y_hbm, ybuf.at[s], sems.at[s], 0, r).wait()
            return carry
        lax.fori_loop(0, 2 * tm, wait, 0, unroll=DMA_UNROLL)

    @pl.when(i == 0)
    def _():
        _gather_rows(idx0_ref, y_hbm, ybuf.at[0], sems.at[0], 2 * tm)

    @pl.when(i >= 1)
    def _():
        wait_rows(slot)

    for r in range(2 * tm):
        _row_copy(y_hbm, ybuf.at[1 - slot], sems.at[1 - slot], idxn_ref[0, 0, r], r).start(
            priority=r % DMA_PRIORITIES)
    gates = gate_ref[...]
    h = gates[:, 0:1] * ybuf[slot, 0:tm, :] + gates[:, 1:2] * ybuf[slot, tm:2 * tm, :]
    y = _layer_norm(alpha * x_ref[...] + h, g_ref[...], b_ref[...], LN_EPS)
    of_ref[...] = y
    ob_ref[...] = y.astype(BF16)

    @pl.when(i == pl.num_programs(0) - 1)
    def _():
        wait_rows(1 - slot)


def moe_combine_ln(y_slots, dest, gates, x, g, b, alpha, tm=256):
    t, d = x.shape
    tm = min(tm, t)
    n_tiles = t // tm
    idx = dest.reshape(n_tiles, tm, 2).transpose(0, 2, 1).reshape(n_tiles, 1, 2 * tm)
    row = lambda i: (i, 0)
    vec = pl.BlockSpec((1, d), lambda i: (0, 0))
    smem_rows = lambda index_map: pl.BlockSpec((1, 1, 2 * tm), index_map, memory_space=pltpu.SMEM)
    return pl.pallas_call(
        functools.partial(_combine_kernel, alpha=alpha),
        grid=(n_tiles,),
        in_specs=[smem_rows(lambda i: (0, 0, 0)),
                  smem_rows(lambda i: (jnp.minimum(i + 1, n_tiles - 1), 0, 0)),
                  pl.BlockSpec(memory_space=pl.ANY),
                  pl.BlockSpec((tm, d), row), pl.BlockSpec((tm, 2), row), vec, vec],
        out_specs=[pl.BlockSpec((tm, d), row), pl.BlockSpec((tm, d), row)],
        out_shape=[jax.ShapeDtypeStruct((t, d), F32), jax.ShapeDtypeStruct((t, d), BF16)],
        scratch_shapes=[pltpu.VMEM((2, 2 * tm, d), F32), pltpu.SemaphoreType.DMA((2,))],
        compiler_params=_params("arbitrary"),
    )(idx, idx, y_slots, x, gates, g.reshape(1, d), b.reshape(1, d))


def moe_swiglu_ln(xf, w_router_pad, w1, w3, w2, g, b, alpha):
    t, d = xf.shape
    routed = route_top2(xf, w_router_pad)
    expert = routed[:, 0:2].astype(jnp.int32)
    gates = routed[:, 2:4]
    flat_e = expert.reshape(-1)
    onehot = (flat_e[:, None] == jnp.arange(N_EXPERTS, dtype=jnp.int32)[None, :]).astype(jnp.int32)
    rank = jnp.sum((jnp.cumsum(onehot, axis=0) - 1) * onehot, axis=1)
    counts = jnp.sum(onehot, axis=0)
    padded = (counts + MOE_ROWS - 1) // MOE_ROWS * MOE_ROWS
    ends = jnp.cumsum(padded)
    dest = (ends - padded)[flat_e] + rank
    n_blocks = (t * 2) // MOE_ROWS + N_EXPERTS
    n_slots = n_blocks * MOE_ROWS
    flat_tok = jnp.arange(t * 2, dtype=jnp.int32) // 2
    slot_tok = jnp.zeros((n_slots,), jnp.int32).at[dest].set(flat_tok)
    block_start = jnp.arange(n_blocks, dtype=jnp.int32) * MOE_ROWS
    block_expert = jnp.minimum(jnp.sum((ends[None, :] <= block_start[:, None]).astype(jnp.int32), axis=1),
                               N_EXPERTS - 1)
    n_used = (ends[N_EXPERTS - 1:] // MOE_ROWS).astype(jnp.int32)
    y_slots = swiglu(xf, block_expert, n_used, w1, w3, w2, g, b, alpha, final_ln=False, slot_rows=slot_tok,
                     tm=MOE_ROWS)
    return moe_combine_ln(y_slots, dest.reshape(t, 2).astype(jnp.int32), gates, xf, g, b, alpha)


def kernel(x, mem, w_in, sg_w, sg_b, sg_ln_g, sg_ln_b, hg_lb_logits, hg_norm_g, rw_mu, rw_w0, rw_w2, rw_a0, rw_a2, rw_g2, rw_k_k, rw_k_a, rw_r_k, rw_gn_g, rw_gn_b, rw_v0, rw_v1, rw_v2, w_branch, w_mix_out, xa_wq, xa_wk, xa_wv, xa_wo, ln_g, ln_b, ffn_w1, ffn_w3, ffn_w2, moe_router, moe_w1, moe_w3, moe_w2):
    batch, seq, d = x.shape
    depth = w_in.shape[0]
    t = batch * seq
    alpha = (2 * depth) ** 0.25
    w = BRANCH_W
    sg_cols, hg_cols = 2 * w, 4 * w
    rw_cols = 3 * w + RW_W_LORA + RW_A_LORA + RW_G_LORA
    o_hg, o_rw, o_gate = sg_cols, sg_cols + hg_cols, sg_cols + hg_cols + rw_cols

    p = jax.nn.softmax(hg_lb_logits.astype(F32), axis=0)
    lower_bounds = jnp.cumsum(p, axis=0) - p[0]
    lane_head = jnp.arange(LANES) // RW_HEAD
    ones_blk = (lane_head[:, None] == lane_head[None, :]).astype(BF16)

    def pad_cols(m, n):
        return jnp.pad(m, ((0, 0), (0, n - m.shape[1])))

    def pad_rows(m, n):
        return jnp.pad(m, ((0, n - m.shape[0]), (0, 0)))

    mem_b = mem.reshape(-1, d).astype(BF16)
    xf = x.reshape(t, d)
    xb = xf.astype(BF16)
    v_first = None
    for l in range(depth):
        wl_ = w_in[l]
        c0 = o_rw + 3 * w
        w_rw = jnp.concatenate([wl_[:, o_rw:c0],
                                pad_cols(wl_[:, c0:c0 + RW_W_LORA], RW_LORA_PAD),
                                pad_cols(wl_[:, c0 + RW_W_LORA:c0 + RW_W_LORA + RW_A_LORA], RW_LORA_PAD),
                                wl_[:, c0 + RW_W_LORA + RW_A_LORA:o_gate]], axis=1)
        w_proj = jnp.concatenate([wl_[:, o_hg:o_rw], w_rw, wl_[:, :o_hg]], axis=1).astype(BF16)
        w_gate = wl_[:, o_gate:].reshape(d, 3, d).transpose(1, 0, 2).astype(BF16)
        mu = rw_mu[l]
        mu_p = jnp.concatenate([mu[:3 * w],
                                jnp.pad(mu[3 * w:3 * w + RW_W_LORA], (0, RW_LORA_PAD - RW_W_LORA)),
                                jnp.pad(mu[3 * w + RW_W_LORA:3 * w + RW_W_LORA + RW_A_LORA],
                                        (0, RW_LORA_PAD - RW_A_LORA)),
                                mu[3 * w + RW_W_LORA + RW_A_LORA:]])

        zall = matmul(xb, w_proj, F32, tn=1024)
        y_sg = sg_branch(zall, 4, sg_w[l], sg_b[l], sg_ln_g[l], sg_ln_b[l])
        y_hg = hgrn2_branch(zall.reshape(batch, seq, -1), 0, lower_bounds[l], hg_norm_g[l]).reshape(t, w)
        vmix = None if l == 0 else (rw_v0[l - 1], rw_v1[l - 1].astype(BF16), rw_v2[l - 1].astype(BF16))
        r_, lw_, k_, v_, kk_, al_, g_ = rw_prep(
            zall, 1, seq, mu_p, rw_w0[l], pad_rows(rw_w2[l], RW_LORA_PAD).astype(BF16), rw_a0[l],
            pad_rows(rw_a2[l], RW_LORA_PAD).astype(BF16), rw_g2[l].astype(BF16), rw_k_k[l], rw_k_a[l],
            ones_blk, vmix, v_first)
        if l == 0:
            v_first = v_
        y_rw = rwkv7_mix(r_, lw_, k_, v_, kk_, al_, g_, rw_gn_g[l], rw_gn_b[l], rw_r_k[l], ones_blk,
                         batch, seq).reshape(t, w)
        merged = merge_branches(xb, y_sg, y_hg, y_rw, w_gate, w_branch[l].astype(BF16))
        xf, xb = matmul_res_ln(merged, w_mix_out[l].astype(BF16), xf, ln_g[l, 0], ln_b[l, 0], alpha)

        kmem = matmul(mem_b, xa_wk[l].astype(BF16), BF16).reshape(batch, -1, XA_HEADS * XA_HEAD_DIM)
        vmem = matmul(mem_b, xa_wv[l].astype(BF16), BF16).reshape(batch, -1, XA_HEADS * XA_HEAD_DIM)
        xf3, xb3 = cross_attention_ln(xb.reshape(batch, seq, d), xf.reshape(batch, seq, d), kmem, vmem,
                                      xa_wq[l].astype(BF16), xa_wo[l].astype(BF16), ln_g[l, 1], ln_b[l, 1], alpha)
        xf, xb = xf3.reshape(t, d), xb3.reshape(t, d)

        if l % 2 == 0:
            i = l // 2
            n_row_blocks = t // min(512, t)
            xf, xb = swiglu(xf, jnp.zeros((n_row_blocks,), jnp.int32), jnp.full((1,), n_row_blocks, jnp.int32),
                            ffn_w1[i:i + 1].astype(BF16),
                            ffn_w3[i:i + 1].astype(BF16), ffn_w2[i:i + 1].astype(BF16),
                            ln_g[l, 2], ln_b[l, 2], alpha, final_ln=True)
        else:
            i = l // 2
            xf, xb = moe_swiglu_ln(xf, pad_cols(moe_router[i], LANES), moe_w1[i].astype(BF16),
                                   moe_w3[i].astype(BF16), moe_w2[i].astype(BF16), ln_g[l, 2], ln_b[l, 2], alpha)
    return xf.reshape(batch, seq, d)
```

```python
import functools
import math

import jax
import jax.numpy as jnp
from jax import lax
from jax.experimental import pallas as pl
from jax.experimental.pallas import tpu as pltpu

F32 = jnp.float32
BF16 = jnp.bfloat16

V7X_VMEM_BYTES = 64 * 1024 * 1024
VMEM_LIMIT = V7X_VMEM_BYTES - 8 * 1024 * 1024
LANES = 128
SUBLANES = 8

BRANCH_W = 512
SG_CHUNK = 128
SG_GROUPS = 4
HG_HEADS = 4
HG_DK = BRANCH_W // HG_HEADS
HG_CHUNK = 16
HG_EPS = 1e-6
HG_EXP_CLIP = 60.0
RW_HEAD = 64
RW_HEADS = BRANCH_W // RW_HEAD
RW_CHUNK = 64
RW_W_LORA = 96
RW_A_LORA = 96
RW_G_LORA = 256
RW_GN_EPS = 64e-5
RW_LORA_PAD = 128
XA_HEADS = 4
XA_HEAD_DIM = 128
N_EXPERTS = 8
MOE_ROWS = 512
FFN_TF = 512
LN_EPS = 1e-5
DMA_UNROLL = 8
DMA_PRIORITIES = 2


def _params(*sem):
    return pltpu.CompilerParams(dimension_semantics=sem, vmem_limit_bytes=VMEM_LIMIT)


def _dot(a, b):
    return jnp.dot(a.astype(BF16), b.astype(BF16), preferred_element_type=F32)


def _dot_nt(a, b):
    return lax.dot_general(a.astype(BF16), b.astype(BF16), (((1,), (1,)), ((), ())),
                           preferred_element_type=F32)


def _dot_tn(a, b):
    return lax.dot_general(a.astype(BF16), b.astype(BF16), (((0,), (0,)), ((), ())),
                           preferred_element_type=F32)


def _split3(x):
    hi = x.astype(BF16)
    r1 = x - hi.astype(F32)
    mid = r1.astype(BF16)
    lo = (r1 - mid.astype(F32)).astype(BF16)
    return hi, mid, lo


def _dot_exact_lhs(m_bf16, x):
    hi, mid, lo = _split3(x)
    return (jnp.dot(m_bf16, hi, preferred_element_type=F32)
            + jnp.dot(m_bf16, mid, preferred_element_type=F32)
            + jnp.dot(m_bf16, lo, preferred_element_type=F32))


def _dot_exact_rhs(x, m_bf16):
    hi, mid, lo = _split3(x)
    return (jnp.dot(hi, m_bf16, preferred_element_type=F32)
            + jnp.dot(mid, m_bf16, preferred_element_type=F32)
            + jnp.dot(lo, m_bf16, preferred_element_type=F32))


def _head_sum(x, ones_blk):
    n = x.shape[-1] // LANES
    return jnp.concatenate(
        [_dot_exact_rhs(x[:, i * LANES:(i + 1) * LANES], ones_blk) for i in range(n)], axis=-1)


def _sigmoid(x):
    return 1.0 / (1.0 + jnp.exp(-x))


def _log_sigmoid(x):
    return jnp.minimum(x, 0.0) - jnp.log1p(jnp.exp(-jnp.abs(x)))


def _layer_norm(x, g, b, eps):
    mu = jnp.mean(x, axis=-1, keepdims=True)
    xc = x - mu
    var = jnp.mean(xc * xc, axis=-1, keepdims=True)
    return xc * lax.rsqrt(var + eps) * g + b


def _tril_mask(n, strict=False):
    row = lax.broadcasted_iota(jnp.int32, (n, n), 0)
    col = lax.broadcasted_iota(jnp.int32, (n, n), 1)
    return (row > col) if strict else (row >= col)


def _mm_kernel(a_ref, b_ref, o_ref):
    o_ref[...] = jnp.dot(a_ref[...], b_ref[...], preferred_element_type=F32).astype(o_ref.dtype)


def matmul(a, b, out_dtype, tm=1024, tn=512):
    m, k = a.shape
    n = b.shape[1]
    tm, tn = min(tm, m), min(tn, n)
    return pl.pallas_call(
        _mm_kernel,
        grid=(m // tm, n // tn),
        in_specs=[pl.BlockSpec((tm, k), lambda i, j: (i, 0)),
                  pl.BlockSpec((k, tn), lambda i, j: (0, j))],
        out_specs=pl.BlockSpec((tm, tn), lambda i, j: (i, j)),
        out_shape=jax.ShapeDtypeStruct((m, n), out_dtype),
        compiler_params=_params("parallel", "parallel"),
    )(a, b)


def _mm_res_ln_kernel(a_ref, w_ref, res_ref, g_ref, b_ref, of_ref, ob_ref, *, alpha):
    h = jnp.dot(a_ref[...], w_ref[...], preferred_element_type=F32)
    y = _layer_norm(alpha * res_ref[...] + h, g_ref[...], b_ref[...], LN_EPS)
    of_ref[...] = y
    ob_ref[...] = y.astype(BF16)


def matmul_res_ln(a, w, res, g, b, alpha, tm=512):
    m, k = a.shape
    n = w.shape[1]
    tm = min(tm, m)
    return pl.pallas_call(
        functools.partial(_mm_res_ln_kernel, alpha=alpha),
        grid=(m // tm,),
        in_specs=[pl.BlockSpec((tm, k), lambda i: (i, 0)),
                  pl.BlockSpec((k, n), lambda i: (0, 0)),
                  pl.BlockSpec((tm, n), lambda i: (i, 0)),
                  pl.BlockSpec((1, n), lambda i: (0, 0)),
                  pl.BlockSpec((1, n), lambda i: (0, 0))],
        out_specs=[pl.BlockSpec((tm, n), lambda i: (i, 0)),
                   pl.BlockSpec((tm, n), lambda i: (i, 0))],
        out_shape=[jax.ShapeDtypeStruct((m, n), F32), jax.ShapeDtypeStruct((m, n), BF16)],
        compiler_params=_params("parallel"),
    )(a, w, res, g.reshape(1, n), b.reshape(1, n))


def _sg_kernel(z_ref, w_ref, b_ref, g_ref, beta_ref, y_ref, *, n_chunks):
    z = z_ref[...]
    z = 0.5 * z * (1.0 + jnp.tanh(math.sqrt(2.0 / math.pi) * (z + 0.044715 * (z * z * z))))
    u = z[:, :BRANCH_W]
    v = _layer_norm(z[:, BRANCH_W:], g_ref[...], beta_ref[...], LN_EPS).astype(BF16)
    causal = _tril_mask(SG_CHUNK)
    gw = BRANCH_W // SG_GROUPS
    for g in range(SG_GROUPS):
        wg = jnp.where(causal, w_ref[g], 0.0).astype(BF16)
        for c in range(n_chunks):
            rows = slice(c * SG_CHUNK, (c + 1) * SG_CHUNK)
            cols = slice(g * gw, (g + 1) * gw)
            s = jnp.dot(wg, v[rows, cols], preferred_element_type=F32) + b_ref[g]
            y_ref[rows, cols] = (u[rows, cols] * s).astype(BF16)


def sg_branch(zall, col_block, sg_w, sg_b, ln_g, ln_b, tm=512):
    t = zall.shape[0]
    tm = min(tm, t)
    return pl.pallas_call(
        functools.partial(_sg_kernel, n_chunks=tm // SG_CHUNK),
        grid=(t // tm,),
        in_specs=[pl.BlockSpec((tm, 2 * BRANCH_W), lambda i: (i, col_block)),
                  pl.BlockSpec((SG_GROUPS, SG_CHUNK, SG_CHUNK), lambda i: (0, 0, 0)),
                  pl.BlockSpec((SG_GROUPS, SG_CHUNK, 1), lambda i: (0, 0, 0)),
                  pl.BlockSpec((1, BRANCH_W), lambda i: (0, 0)),
                  pl.BlockSpec((1, BRANCH_W), lambda i: (0, 0))],
        out_specs=pl.BlockSpec((tm, BRANCH_W), lambda i: (i, 0)),
        out_shape=jax.ShapeDtypeStruct((t, BRANCH_W), BF16),
        compiler_params=_params("parallel"),
    )(zall, sg_w, sg_b.reshape(SG_GROUPS, SG_CHUNK, 1), ln_g.reshape(1, -1), ln_b.reshape(1, -1))


def _hgrn_kernel(z_ref, lb_ref, ng_ref, y_ref, st_ref, *, batch):
    c = HG_CHUNK

    @pl.when(pl.program_id(0) == 0)
    def _():
        st_ref[...] = jnp.zeros_like(st_ref)

    lb = lb_ref[...]
    ltri = jnp.where(_tril_mask(c), 1.0, 0.0).astype(BF16)
    rows = lax.broadcasted_iota(jnp.int32, (c, 1), 0)
    w = BRANCH_W
    for b in range(batch):
        zq = z_ref[b, :, 0:w]
        zf = z_ref[b, :, w:2 * w]
        zi = z_ref[b, :, 2 * w:3 * w]
        zg = z_ref[b, :, 3 * w:4 * w]
        log_f = _log_sigmoid(zf) + jnp.log1p(lb * jnp.exp(jnp.minimum(-zf, HG_EXP_CLIP)))
        log_f = jnp.minimum(log_f, 0.0)
        kx = (1.0 - lb) * _sigmoid(-zf)
        q = zq * _sigmoid(zq)
        bc = _dot_exact_lhs(ltri, log_f)
        b_last = bc[c - 1:c, :]
        qd = q * jnp.exp(bc)
        kd = kx * jnp.exp(b_last - bc)
        e_last = jnp.exp(b_last)
        outs = []
        for h in range(HG_HEADS):
            sl = slice(h * HG_DK, (h + 1) * HG_DK)
            st = st_ref[b * HG_HEADS + h]
            o = _dot_nt(qd[:, sl], st)
            qh, kh, bh, vh = q[:, sl], kx[:, sl], bc[:, sl], zi[:, sl]
            parts = []
            for r0 in range(0, c, SUBLANES):
                rs = slice(r0, r0 + SUBLANES)
                qg, bg, og, rg = qh[rs], bh[rs], o[rs], rows[rs]
                for s in range(r0 + SUBLANES):
                    m = qg * kh[s:s + 1, :] * jnp.exp(jnp.minimum(bg - bh[s:s + 1, :], 0.0))
                    att = jnp.sum(m, axis=-1, keepdims=True)
                    if s > r0:
                        att = jnp.where(rg >= s, att, 0.0)
                    og = og + att * vh[s:s + 1, :]
                parts.append(og)
            o = jnp.concatenate(parts, axis=0)
            st_ref[b * HG_HEADS + h] = st * e_last[:, sl] + _dot_tn(vh, kd[:, sl])
            outs.append(o)
        o = jnp.concatenate(outs, axis=-1) * _sigmoid(zg)
        y = o * lax.rsqrt(jnp.mean(o * o, axis=-1, keepdims=True) + HG_EPS) * ng_ref[...]
        y_ref[b] = y.astype(BF16)


def hgrn2_branch(z3, col_block, lb, norm_g):
    batch, seq, _ = z3.shape
    return pl.pallas_call(
        functools.partial(_hgrn_kernel, batch=batch),
        grid=(seq // HG_CHUNK,),
        in_specs=[pl.BlockSpec((batch, HG_CHUNK, 4 * BRANCH_W), lambda c: (0, c, col_block)),
                  pl.BlockSpec((1, BRANCH_W), lambda c: (0, 0)),
                  pl.BlockSpec((1, BRANCH_W), lambda c: (0, 0))],
        out_specs=pl.BlockSpec((batch, HG_CHUNK, BRANCH_W), lambda c: (0, c, 0)),
        out_shape=jax.ShapeDtypeStruct((batch, seq, BRANCH_W), BF16),
        scratch_shapes=[pltpu.VMEM((batch * HG_HEADS, HG_DK, HG_DK), F32)],
        compiler_params=_params("arbitrary"),
    )(z3, lb.reshape(1, -1), norm_g.reshape(1, -1))


def _rw_prep_kernel(*refs, tiles_per_seq, has_vmix):
    if has_vmix:
        (z_ref, zp_ref, mu_ref, w0_ref, w2_ref, a0_ref, a2_ref, g2_ref, kk_ref, ka_ref, ones_ref,
         v0_ref, v1_ref, v2_ref, vf_ref, r_o, lw_o, k_o, v_o, kk_o, al_o, g_o) = refs
    else:
        (z_ref, zp_ref, mu_ref, w0_ref, w2_ref, a0_ref, a2_ref, g2_ref, kk_ref, ka_ref, ones_ref,
         r_o, lw_o, k_o, v_o, kk_o, al_o, g_o) = refs
    z = z_ref[...]
    tm = z.shape[0]
    first = (pl.program_id(0) % tiles_per_seq) == 0
    prev = jnp.where(first, 0.0, zp_ref[7:8, :])
    rows = lax.broadcasted_iota(jnp.int32, (tm, 1), 0)
    zs = jnp.where(rows == 0, prev, pltpu.roll(z, 1, 0))
    zc = z + (zs - z) * mu_ref[...]
    w = BRANCH_W
    p = RW_LORA_PAD
    r, k, v = zc[:, 0:w], zc[:, w:2 * w], zc[:, 2 * w:3 * w]
    wl, al_in, gl = zc[:, 3 * w:3 * w + p], zc[:, 3 * w + p:3 * w + 2 * p], zc[:, 3 * w + 2 * p:]
    u = w0_ref[...] + _dot(jnp.tanh(wl), w2_ref[...])
    lw = -math.exp(-0.5) * _sigmoid(u)
    a = _sigmoid(a0_ref[...] + _dot(al_in, a2_ref[...]))
    g = _dot(_sigmoid(gl), g2_ref[...])
    if has_vmix:
        mix = _sigmoid(v0_ref[...] + _dot(_dot(v, v1_ref[...]), v2_ref[...]))
        v = v + (vf_ref[...] - v) * mix
    kkx = k * kk_ref[...]
    norm = jnp.sqrt(_head_sum(kkx * kkx, ones_ref[...]))
    kkn = kkx / jnp.maximum(norm, 1e-12)
    r_o[...] = r
    lw_o[...] = lw
    k_o[...] = k * (1.0 + (a - 1.0) * ka_ref[...])
    v_o[...] = v
    kk_o[...] = kkn
    al_o[...] = a
    g_o[...] = g


def rw_prep(zall, col_block, seq, mu, w0, w2p, a0, a2p, g2, k_k, k_a, ones_blk, vmix, v_first, tm=256):
    t = zall.shape[0]
    tm = min(tm, seq)
    zw = 4 * BRANCH_W
    row = lambda i: (i, 0)
    const = lambda i: (0, 0)
    vec = pl.BlockSpec((1, BRANCH_W), const)
    in_specs = [pl.BlockSpec((tm, zw), lambda i: (i, col_block)),
                pl.BlockSpec((8, zw), lambda i: (jnp.maximum(i * (tm // 8) - 1, 0), col_block)),
                pl.BlockSpec((1, zw), const), vec,
                pl.BlockSpec((RW_LORA_PAD, BRANCH_W), const), vec,
                pl.BlockSpec((RW_LORA_PAD, BRANCH_W), const),
                pl.BlockSpec((RW_G_LORA, BRANCH_W), const), vec, vec,
                pl.BlockSpec((LANES, LANES), const)]
    args = [zall, zall, mu.reshape(1, -1), w0.reshape(1, -1), w2p, a0.reshape(1, -1), a2p, g2,
            k_k.reshape(1, -1), k_a.reshape(1, -1), ones_blk]
    if vmix is not None:
        v0, v1, v2 = vmix
        in_specs += [vec, pl.BlockSpec(v1.shape, const), pl.BlockSpec(v2.shape, const),
                     pl.BlockSpec((tm, BRANCH_W), row)]
        args += [v0.reshape(1, -1), v1, v2, v_first]
    out = jax.ShapeDtypeStruct((t, BRANCH_W), F32)
    return pl.pallas_call(
        functools.partial(_rw_prep_kernel, tiles_per_seq=seq // tm, has_vmix=vmix is not None),
        grid=(t // tm,),
        in_specs=in_specs,
        out_specs=[pl.BlockSpec((tm, BRANCH_W), row)] * 7,
        out_shape=[out] * 7,
        compiler_params=_params("parallel"),
    )(*args)


def _rwkv_kernel(r_ref, lw_ref, k_ref, v_ref, kk_ref, al_ref, g_ref, gng_ref, gnb_ref, rk_ref, ones_ref,
                 y_ref, st_ref, *, batch):
    c = RW_CHUNK

    @pl.when(pl.program_id(0) == 0)
    def _():
        st_ref[...] = jnp.zeros_like(st_ref)

    incl = _tril_mask(c)
    strict = _tril_mask(c, strict=True)
    ltri = jnp.where(incl, 1.0, 0.0).astype(BF16)
    units = [(b, h) for b in range(batch) for h in range(RW_HEADS)]
    ar, bk, vv, st, g_last = {}, {}, {}, {}, {}
    for b in range(batch):
        lw, kk = lw_ref[b], kk_ref[b]
        cum = _dot_exact_lhs(ltri, lw)
        gam = jnp.exp(cum)
        ginv = jnp.exp(-cum)
        at = -kk * jnp.exp(cum - lw)
        rt = r_ref[b] * gam
        bt = kk * al_ref[b] * ginv
        kt = k_ref[b] * ginv
        v = v_ref[b]
        for h in range(RW_HEADS):
            sl = slice(h * RW_HEAD, (h + 1) * RW_HEAD)
            ar[b, h] = jnp.concatenate([at[:, sl], rt[:, sl]], axis=0).astype(BF16)
            bk[b, h] = jnp.concatenate([bt[:, sl], kt[:, sl]], axis=0).astype(BF16)
            vv[b, h] = v[:, sl]
            st[b, h] = st_ref[b * RW_HEADS + h]
            g_last[b, h] = gam[c - 1:c, sl]
    nb = {u: _dot_nt(ar[u], bk[u][:c]) for u in units}
    nk = {u: _dot_nt(ar[u], bk[u][c:]) for u in units}
    a_s = {u: _dot_nt(ar[u], st[u]) for u in units}
    n = {u: jnp.where(strict, nb[u][:c], 0.0).astype(BF16) for u in units}
    rb = {u: jnp.where(incl, nb[u][c:], 0.0) for u in units}
    akrk = {u: jnp.concatenate([jnp.where(strict, nk[u][:c], 0.0), jnp.where(incl, nk[u][c:], 0.0)], axis=0)
            for u in units}
    kv = {u: _dot(akrk[u], vv[u]) for u in units}
    x = {u: a_s[u][:c] + kv[u][:c] for u in units}
    uu = {u: x[u] + _dot(n[u], x[u]) for u in units}
    p = n
    for _ in range(int(math.log2(c)) - 1):
        p = {u: _dot(p[u], p[u]).astype(BF16) for u in units}
        uu = {u: uu[u] + _dot(p[u], uu[u]) for u in units}
    y = {u: a_s[u][c:] + kv[u][c:] + _dot(rb[u], uu[u]) for u in units}
    for u in units:
        upd = _dot_tn(jnp.concatenate([uu[u], vv[u]], axis=0), bk[u])
        st_ref[u[0] * RW_HEADS + u[1]] = (st[u] + upd) * g_last[u]
    ones = ones_ref[...]
    inv_n = 1.0 / RW_HEAD
    for b in range(batch):
        yb = jnp.concatenate([y[b, h] for h in range(RW_HEADS)], axis=-1)
        mu_y = _head_sum(yb, ones) * inv_n
        yc = yb - mu_y
        var = _head_sum(yc * yc, ones) * inv_n
        yn = yc * lax.rsqrt(var + RW_GN_EPS) * gng_ref[...] + gnb_ref[...]
        bonus = _head_sum(r_ref[b] * k_ref[b] * rk_ref[...], ones)
        y_ref[b] = ((yn + bonus * v_ref[b]) * g_ref[b]).astype(BF16)


def rwkv7_mix(r, lw, k, v, kk, al, g, gn_g, gn_b, r_k, ones_blk, batch, seq):
    arrs = [x.reshape(batch, seq, BRANCH_W) for x in (r, lw, k, v, kk, al, g)]
    blk = pl.BlockSpec((batch, RW_CHUNK, BRANCH_W), lambda c: (0, c, 0))
    vec = pl.BlockSpec((1, BRANCH_W), lambda c: (0, 0))
    return pl.pallas_call(
        functools.partial(_rwkv_kernel, batch=batch),
        grid=(seq // RW_CHUNK,),
        in_specs=[blk] * 7 + [vec] * 3 + [pl.BlockSpec((LANES, LANES), lambda c: (0, 0))],
        out_specs=blk,
        out_shape=jax.ShapeDtypeStruct((batch, seq, BRANCH_W), BF16),
        scratch_shapes=[pltpu.VMEM((batch * RW_HEADS, RW_HEAD, RW_HEAD), F32)],
        compiler_params=_params("arbitrary"),
    )(*arrs, gn_g.reshape(1, -1), gn_b.reshape(1, -1), r_k.reshape(1, -1), ones_blk)


def _merge_kernel(x_ref, ysg_ref, yhg_ref, yrw_ref, wg_ref, wb_ref, o_ref):
    x = x_ref[...]
    acc = None
    for b, y_ref in enumerate((ysg_ref, yhg_ref, yrw_ref)):
        gate = _sigmoid(jnp.dot(x, wg_ref[b], preferred_element_type=F32))
        term = gate * jnp.dot(y_ref[...], wb_ref[b], preferred_element_type=F32)
        acc = term if acc is None else acc + term
    o_ref[...] = acc.astype(BF16)


def merge_branches(xb, y_sg, y_hg, y_rw, w_gate, w_branch, tm=1024, tn=512):
    t, d = xb.shape
    tm = min(tm, t)
    ysp = pl.BlockSpec((tm, BRANCH_W), lambda i, j: (i, 0))
    return pl.pallas_call(
        _merge_kernel,
        grid=(t // tm, d // tn),
        in_specs=[pl.BlockSpec((tm, d), lambda i, j: (i, 0)), ysp, ysp, ysp,
                  pl.BlockSpec((3, d, tn), lambda i, j: (0, 0, j)),
                  pl.BlockSpec((3, BRANCH_W, tn), lambda i, j: (0, 0, j))],
        out_specs=pl.BlockSpec((tm, tn), lambda i, j: (i, j)),
        out_shape=jax.ShapeDtypeStruct((t, d), BF16),
        compiler_params=_params("parallel", "parallel"),
    )(xb, y_sg, y_hg, y_rw, w_gate, w_branch)


def _xattn_kernel(xb_ref, xf_ref, k_ref, v_ref, wq_ref, wo_ref, g_ref, b_ref, of_ref, ob_ref, *, alpha):
    q = jnp.dot(xb_ref[0], wq_ref[...], preferred_element_type=F32)
    kmem, vmem = k_ref[0], v_ref[0]
    scale = XA_HEAD_DIM ** -0.5
    outs = []
    for h in range(XA_HEADS):
        sl = slice(h * XA_HEAD_DIM, (h + 1) * XA_HEAD_DIM)
        s = _dot_nt(q[:, sl], kmem[:, sl]) * scale
        e = jnp.exp(s - jnp.max(s, axis=-1, keepdims=True))
        p = e / jnp.sum(e, axis=-1, keepdims=True)
        outs.append(_dot(p, vmem[:, sl]))
    o = jnp.concatenate(outs, axis=-1)
    h_out = _dot(o, wo_ref[...])
    y = _layer_norm(alpha * xf_ref[0] + h_out, g_ref[...], b_ref[...], LN_EPS)
    of_ref[0] = y
    ob_ref[0] = y.astype(BF16)


def cross_attention_ln(xb, xf, kmem, vmem, wq, wo, g, b, alpha, tm=512):
    batch, seq, d = xf.shape
    m, xw = kmem.shape[1], kmem.shape[2]
    tm = min(tm, seq)
    xs = pl.BlockSpec((1, tm, d), lambda bi, i: (bi, i, 0))
    ms = pl.BlockSpec((1, m, xw), lambda bi, i: (bi, 0, 0))
    vec = pl.BlockSpec((1, d), lambda bi, i: (0, 0))
    return pl.pallas_call(
        functools.partial(_xattn_kernel, alpha=alpha),
        grid=(batch, seq // tm),
        in_specs=[xs, xs, ms, ms,
                  pl.BlockSpec((d, xw), lambda bi, i: (0, 0)),
                  pl.BlockSpec((xw, d), lambda bi, i: (0, 0)), vec, vec],
        out_specs=[xs, xs],
        out_shape=[jax.ShapeDtypeStruct((batch, seq, d), F32), jax.ShapeDtypeStruct((batch, seq, d), BF16)],
        compiler_params=_params("parallel", "parallel"),
    )(xb, xf, kmem, vmem, wq, wo, g.reshape(1, d), b.reshape(1, d))


def _ffn_kernel(be_ref, nu_ref, *refs, alpha, final_ln, gather, nf):
    del be_ref
    if gather:
        idx0_ref, idxn_ref, x_ref, w13_ref, w2_ref, g_ref, b_ref, of_ref, xb_scr, acc_scr, xg_scr, sems = refs
    elif final_ln:
        x_ref, w13_ref, w2_ref, g_ref, b_ref, of_ref, ob_ref, xb_scr, acc_scr = refs
    else:
        x_ref, w13_ref, w2_ref, g_ref, b_ref, of_ref, xb_scr, acc_scr = refs
    i = pl.program_id(0)
    f = pl.program_id(1)
    last = f == nf - 1
    n_used = nu_ref[0]
    active = i < n_used
    tm = xb_scr.shape[0]
    per_step = -(-tm // nf)
    slot = i % 2

    if gather:
        @pl.when((i == 0) & (f == 0))
        def _():
            _gather_rows(idx0_ref, x_ref, xg_scr.at[0], sems.at[0], tm)

        @pl.when((f == 0) & (i >= 1) & (i <= n_used))
        def _():
            def wait(r, carry):
                _row_copy(x_ref, xg_scr.at[slot], sems.at[slot], 0, r).wait()
                return carry
            lax.fori_loop(0, per_step * nf, wait, 0, unroll=DMA_UNROLL)

    @pl.when(active & (f == 0))
    def _():
        if gather:
            xb_scr[...] = xg_scr[slot, 0:tm, :].astype(BF16)
        else:
            xb_scr[...] = x_ref[...].astype(BF16)
        acc_scr[...] = jnp.zeros_like(acc_scr)

    @pl.when(active)
    def _():
        if gather:
            for j in range(per_step):
                r = f * per_step + j
                src_row = idxn_ref[0, 0, jnp.minimum(r, tm - 1)]
                _row_copy(x_ref, xg_scr.at[1 - slot], sems.at[1 - slot], src_row, r).start(priority=1)
        xb = xb_scr[...]
        gate_up = jnp.dot(xb, w13_ref[0, 0], preferred_element_type=F32)
        tf = gate_up.shape[1] // 2
        gate, up = gate_up[:, :tf], gate_up[:, tf:]
        hh = (gate * _sigmoid(gate) * up).astype(BF16)
        acc_scr[...] += jnp.dot(hh, w2_ref[0], preferred_element_type=F32)

    @pl.when(active & last)
    def _():
        if final_ln:
            y = _layer_norm(alpha * x_ref[...] + acc_scr[...], g_ref[...], b_ref[...], LN_EPS)
            of_ref[...] = y
            ob_ref[...] = y.astype(BF16)
        else:
            of_ref[...] = acc_scr[...]

    @pl.when(jnp.logical_not(active) & last)
    def _():
        of_ref[...] = jnp.zeros_like(of_ref)
        if final_ln:
            ob_ref[...] = jnp.zeros_like(ob_ref)


def fuse_up_weights(w1, w3, tf=FFN_TF):
    e, d, ff = w1.shape
    tiles = lambda w: w.reshape(e, d, ff // tf, tf).transpose(0, 2, 1, 3)
    return jnp.concatenate([tiles(w1), tiles(w3)], axis=-1).astype(BF16)


def swiglu(x, block_expert, n_used, w13, w2, g, b, alpha, final_ln, slot_rows=None, tm=512):
    gather = slot_rows is not None
    assert not (gather and final_ln)
    d = x.shape[1]
    rows = slot_rows.shape[0] if gather else x.shape[0]
    nf, tf = w13.shape[1], w13.shape[3] // 2
    tm = min(tm, rows)
    n_blocks = rows // tm
    fi = lambda i, f, nu: jnp.where(i < nu[0], f, nf - 1)
    xs = pl.BlockSpec((tm, d), lambda i, f, be, nu: (i, 0))
    vec = pl.BlockSpec((1, d), lambda i, f, be, nu: (0, 0))
    weights = [pl.BlockSpec((1, 1, d, 2 * tf), lambda i, f, be, nu: (be[i], fi(i, f, nu), 0, 0)),
               pl.BlockSpec((1, tf, d), lambda i, f, be, nu: (be[i], fi(i, f, nu), 0)),
               vec, vec]
    scratch = [pltpu.VMEM((tm, d), BF16), pltpu.VMEM((tm, d), F32)]
    if gather:
        idx = slot_rows.reshape(n_blocks, 1, tm)
        smem_rows = lambda index_map: pl.BlockSpec((1, 1, tm), index_map, memory_space=pltpu.SMEM)
        in_specs = [smem_rows(lambda i, f, be, nu: (0, 0, 0)),
                    smem_rows(lambda i, f, be, nu: (jnp.minimum(i + 1, n_blocks - 1), 0, 0)),
                    pl.BlockSpec(memory_space=pl.ANY)] + weights
        args = (idx, idx, x)
        fetch_rows = -(-tm // nf) * nf
        scratch += [pltpu.VMEM((2, -(-fetch_rows // SUBLANES) * SUBLANES, d), F32), pltpu.SemaphoreType.DMA((2,))]
    else:
        in_specs = [pl.BlockSpec((tm, d), lambda i, f, be, nu: (jnp.minimum(i, nu[0] - 1), 0))] + weights
        args = (x,)
    n_out = 2 if final_ln else 1
    out_shape = [jax.ShapeDtypeStruct((rows, d), F32), jax.ShapeDtypeStruct((rows, d), BF16)][:n_out]
    res = pl.pallas_call(
        functools.partial(_ffn_kernel, alpha=alpha, final_ln=final_ln, gather=gather, nf=nf),
        grid_spec=pltpu.PrefetchScalarGridSpec(
            num_scalar_prefetch=2,
            grid=(n_blocks, nf),
            in_specs=in_specs,
            out_specs=[xs] * n_out,
            scratch_shapes=scratch),
        out_shape=out_shape,
        compiler_params=_params("arbitrary" if gather else "parallel", "arbitrary"),
    )(block_expert, n_used, *args, w13, w2, g.reshape(1, d), b.reshape(1, d))
    return res if final_ln else res[0]


def _router_kernel(x_ref, w_ref, o_ref):
    x = x_ref[...]
    hi, mid, lo = _split3(x)
    whi, wmid, wlo = _split3(w_ref[...])
    dot = lambda a, b: jnp.dot(a, b, preferred_element_type=F32)
    logits = (dot(hi, whi) + (dot(hi, wmid) + dot(mid, whi))
              + (dot(mid, wmid) + dot(hi, wlo) + dot(lo, whi)))
    lane = lax.broadcasted_iota(jnp.int32, logits.shape, 1)
    neg = -jnp.inf
    lg = jnp.where(lane < N_EXPERTS, logits, neg)
    m1 = jnp.max(lg, axis=-1, keepdims=True)
    i1 = jnp.min(jnp.where(lg == m1, lane, LANES), axis=-1, keepdims=True)
    lg2 = jnp.where(lane == i1, neg, lg)
    m2 = jnp.max(lg2, axis=-1, keepdims=True)
    i2 = jnp.min(jnp.where(lg2 == m2, lane, LANES), axis=-1, keepdims=True)
    e = jnp.exp(m2 - m1)
    g1 = 1.0 / (1.0 + e)
    g2 = e / (1.0 + e)
    out = jnp.where(lane == 0, i1.astype(F32),
                    jnp.where(lane == 1, i2.astype(F32),
                              jnp.where(lane == 2, g1, jnp.where(lane == 3, g2, 0.0))))
    o_ref[...] = out


def route_top2(x, w_router_pad, tm=1024):
    t, d = x.shape
    tm = min(tm, t)
    return pl.pallas_call(
        _router_kernel,
        grid=(t // tm,),
        in_specs=[pl.BlockSpec((tm, d), lambda i: (i, 0)), pl.BlockSpec((d, LANES), lambda i: (0, 0))],
        out_specs=pl.BlockSpec((tm, LANES), lambda i: (i, 0)),
        out_shape=jax.ShapeDtypeStruct((t, LANES), F32),
        compiler_params=_params("parallel"),
    )(x, w_router_pad)


def _row_copy(src_hbm, dst_ref, sem, src_row, dst_row):
    return pltpu.make_async_copy(src_hbm.at[pl.ds(src_row, 1)], dst_ref.at[pl.ds(dst_row, 1)], sem)


def _gather_rows(idx_ref, src_hbm, dst_ref, sem, rows):
    def start(i, carry):
        for p in range(DMA_PRIORITIES):
            r = i * DMA_PRIORITIES + p
            _row_copy(src_hbm, dst_ref, sem, idx_ref[0, 0, r], r).start(priority=p)
        return carry

    def wait(r, carry):
        _row_copy(src_hbm, dst_ref, sem, 0, r).wait()
        return carry

    lax.fori_loop(0, rows // DMA_PRIORITIES, start, 0, unroll=DMA_UNROLL // DMA_PRIORITIES)
    lax.fori_loop(0, rows, wait, 0, unroll=DMA_UNROLL)


def _combine_kernel(idx0_ref, idxn_ref, y_hbm, x_ref, gate_ref, g_ref, b_ref, of_ref, ob_ref, ybuf, sems, *, alpha):
    tm = x_ref.shape[0]
    i = pl.program_id(0)
    slot = i % 2

    def wait_rows(s):
        def wait(r, carry):
            _row_copy(y_hbm, ybuf.at[s], sems.at[s], 0, r).wait()
            return carry
        lax.fori_loop(0, 2 * tm, wait, 0, unroll=DMA_UNROLL)

    @pl.when(i == 0)
    def _():
        _gather_rows(idx0_ref, y_hbm, ybuf.at[0], sems.at[0], 2 * tm)

    @pl.when(i >= 1)
    def _():
        wait_rows(slot)

    for r in range(2 * tm):
        _row_copy(y_hbm, ybuf.at[1 - slot], sems.at[1 - slot], idxn_ref[0, 0, r], r).start(
            priority=r % DMA_PRIORITIES)
    gates = gate_ref[...]
    h = gates[:, 0:1] * ybuf[slot, 0:tm, :] + gates[:, 1:2] * ybuf[slot, tm:2 * tm, :]
    y = _layer_norm(alpha * x_ref[...] + h, g_ref[...], b_ref[...], LN_EPS)
    of_ref[...] = y
    ob_ref[...] = y.astype(BF16)

    @pl.when(i == pl.num_programs(0) - 1)
    def _():
        wait_rows(1 - slot)


def moe_combine_ln(y_slots, dest, gates, x, g, b, alpha, tm=256):
    t, d = x.shape
    tm = min(tm, t)
    n_tiles = t // tm
    idx = dest.reshape(n_tiles, tm, 2).transpose(0, 2, 1).reshape(n_tiles, 1, 2 * tm)
    row = lambda i: (i, 0)
    vec = pl.BlockSpec((1, d), lambda i: (0, 0))
    smem_rows = lambda index_map: pl.BlockSpec((1, 1, 2 * tm), index_map, memory_space=pltpu.SMEM)
    return pl.pallas_call(
        functools.partial(_combine_kernel, alpha=alpha),
        grid=(n_tiles,),
        in_specs=[smem_rows(lambda i: (0, 0, 0)),
                  smem_rows(lambda i: (jnp.minimum(i + 1, n_tiles - 1), 0, 0)),
                  pl.BlockSpec(memory_space=pl.ANY),
                  pl.BlockSpec((tm, d), row), pl.BlockSpec((tm, 2), row), vec, vec],
        out_specs=[pl.BlockSpec((tm, d), row), pl.BlockSpec((tm, d), row)],
        out_shape=[jax.ShapeDtypeStruct((t, d), F32), jax.ShapeDtypeStruct((t, d), BF16)],
        scratch_shapes=[pltpu.VMEM((2, 2 * tm, d), F32), pltpu.SemaphoreType.DMA((2,))],
        compiler_params=_params("arbitrary"),
    )(idx, idx, y_slots, x, gates, g.reshape(1, d), b.reshape(1, d))


def moe_swiglu_ln(xf, w_router_pad, w13, w2, g, b, alpha):
    t, d = xf.shape
    routed = route_top2(xf, w_router_pad)
    expert = routed[:, 0:2].astype(jnp.int32)
    gates = routed[:, 2:4]
    flat_e = expert.reshape(-1)
    onehot = (flat_e[:, None] == jnp.arange(N_EXPERTS, dtype=jnp.int32)[None, :]).astype(jnp.int32)
    rank = jnp.sum((jnp.cumsum(onehot, axis=0) - 1) * onehot, axis=1)
    counts = jnp.sum(onehot, axis=0)
    padded = (counts + MOE_ROWS - 1) // MOE_ROWS * MOE_ROWS
    ends = jnp.cumsum(padded)
    dest = (ends - padded)[flat_e] + rank
    n_blocks = (t * 2) // MOE_ROWS + N_EXPERTS
    n_slots = n_blocks * MOE_ROWS
    flat_tok = jnp.arange(t * 2, dtype=jnp.int32) // 2
    slot_tok = jnp.zeros((n_slots,), jnp.int32).at[dest].set(flat_tok)
    block_start = jnp.arange(n_blocks, dtype=jnp.int32) * MOE_ROWS
    block_expert = jnp.minimum(jnp.sum((ends[None, :] <= block_start[:, None]).astype(jnp.int32), axis=1),
                               N_EXPERTS - 1)
    n_used = (ends[N_EXPERTS - 1:] // MOE_ROWS).astype(jnp.int32)
    y_slots = swiglu(xf, block_expert, n_used, w13, w2, g, b, alpha, final_ln=False, slot_rows=slot_tok,
                     tm=MOE_ROWS)
    return moe_combine_ln(y_slots, dest.reshape(t, 2).astype(jnp.int32), gates, xf, g, b, alpha)


def kernel(x, mem, w_in, sg_w, sg_b, sg_ln_g, sg_ln_b, hg_lb_logits, hg_norm_g, rw_mu, rw_w0, rw_w2, rw_a0, rw_a2, rw_g2, rw_k_k, rw_k_a, rw_r_k, rw_gn_g, rw_gn_b, rw_v0, rw_v1, rw_v2, w_branch, w_mix_out, xa_wq, xa_wk, xa_wv, xa_wo, ln_g, ln_b, ffn_w1, ffn_w3, ffn_w2, moe_router, moe_w1, moe_w3, moe_w2):
    batch, seq, d = x.shape
    depth = w_in.shape[0]
    t = batch * seq
    alpha = (2 * depth) ** 0.25
    w = BRANCH_W
    sg_cols, hg_cols = 2 * w, 4 * w
    rw_cols = 3 * w + RW_W_LORA + RW_A_LORA + RW_G_LORA
    o_hg, o_rw, o_gate = sg_cols, sg_cols + hg_cols, sg_cols + hg_cols + rw_cols

    p = jax.nn.softmax(hg_lb_logits.astype(F32), axis=0)
    lower_bounds = jnp.cumsum(p, axis=0) - p[0]
    lane_head = jnp.arange(LANES) // RW_HEAD
    ones_blk = (lane_head[:, None] == lane_head[None, :]).astype(BF16)

    def pad_cols(m, n):
        return jnp.pad(m, ((0, 0), (0, n - m.shape[1])))

    def pad_rows(m, n):
        return jnp.pad(m, ((0, n - m.shape[0]), (0, 0)))

    mem_b = mem.reshape(-1, d).astype(BF16)
    xf = x.reshape(t, d)
    xb = xf.astype(BF16)
    v_first = None
    for l in range(depth):
        wl_ = w_in[l]
        c0 = o_rw + 3 * w
        w_rw = jnp.concatenate([wl_[:, o_rw:c0],
                                pad_cols(wl_[:, c0:c0 + RW_W_LORA], RW_LORA_PAD),
                                pad_cols(wl_[:, c0 + RW_W_LORA:c0 + RW_W_LORA + RW_A_LORA], RW_LORA_PAD),
                                wl_[:, c0 + RW_W_LORA + RW_A_LORA:o_gate]], axis=1)
        w_proj = jnp.concatenate([wl_[:, o_hg:o_rw], w_rw, wl_[:, :o_hg]], axis=1).astype(BF16)
        w_gate = wl_[:, o_gate:].reshape(d, 3, d).transpose(1, 0, 2).astype(BF16)
        mu = rw_mu[l]
        mu_p = jnp.concatenate([mu[:3 * w],
                                jnp.pad(mu[3 * w:3 * w + RW_W_LORA], (0, RW_LORA_PAD - RW_W_LORA)),
                                jnp.pad(mu[3 * w + RW_W_LORA:3 * w + RW_W_LORA + RW_A_LORA],
                                        (0, RW_LORA_PAD - RW_A_LORA)),
                                mu[3 * w + RW_W_LORA + RW_A_LORA:]])

        zall = matmul(xb, w_proj, F32, tn=1024)
        y_sg = sg_branch(zall, 4, sg_w[l], sg_b[l], sg_ln_g[l], sg_ln_b[l])
        y_hg = hgrn2_branch(zall.reshape(batch, seq, -1), 0, lower_bounds[l], hg_norm_g[l]).reshape(t, w)
        vmix = None if l == 0 else (rw_v0[l - 1], rw_v1[l - 1].astype(BF16), rw_v2[l - 1].astype(BF16))
        r_, lw_, k_, v_, kk_, al_, g_ = rw_prep(
            zall, 1, seq, mu_p, rw_w0[l], pad_rows(rw_w2[l], RW_LORA_PAD).astype(BF16), rw_a0[l],
            pad_rows(rw_a2[l], RW_LORA_PAD).astype(BF16), rw_g2[l].astype(BF16), rw_k_k[l], rw_k_a[l],
            ones_blk, vmix, v_first)
        if l == 0:
            v_first = v_
        y_rw = rwkv7_mix(r_, lw_, k_, v_, kk_, al_, g_, rw_gn_g[l], rw_gn_b[l], rw_r_k[l], ones_blk,
                         batch, seq).reshape(t, w)
        merged = merge_branches(xb, y_sg, y_hg, y_rw, w_gate, w_branch[l].astype(BF16))
        xf, xb = matmul_res_ln(merged, w_mix_out[l].astype(BF16), xf, ln_g[l, 0], ln_b[l, 0], alpha)

        kmem = matmul(mem_b, xa_wk[l].astype(BF16), BF16).reshape(batch, -1, XA_HEADS * XA_HEAD_DIM)
        vmem = matmul(mem_b, xa_wv[l].astype(BF16), BF16).reshape(batch, -1, XA_HEADS * XA_HEAD_DIM)
        xf3, xb3 = cross_attention_ln(xb.reshape(batch, seq, d), xf.reshape(batch, seq, d), kmem, vmem,
                                      xa_wq[l].astype(BF16), xa_wo[l].astype(BF16), ln_g[l, 1], ln_b[l, 1], alpha)
        xf, xb = xf3.reshape(t, d), xb3.reshape(t, d)

        if l % 2 == 0:
            i = l // 2
            n_row_blocks = t // min(512, t)
            xf, xb = swiglu(xf, jnp.zeros((n_row_blocks,), jnp.int32), jnp.full((1,), n_row_blocks, jnp.int32),
                            fuse_up_weights(ffn_w1[i:i + 1], ffn_w3[i:i + 1]), ffn_w2[i:i + 1].astype(BF16),
                            ln_g[l, 2], ln_b[l, 2], alpha, final_ln=True)
        else:
            i = l // 2
            xf, xb = moe_swiglu_ln(xf, pad_cols(moe_router[i], LANES), fuse_up_weights(moe_w1[i], moe_w3[i]),
                                   moe_w2[i].astype(BF16), ln_g[l, 2], ln_b[l, 2], alpha)
    return xf.reshape(batch, seq, d)
```

```python
import functools
import math

import jax
import jax.numpy as jnp
from jax import lax
from jax.experimental import pallas as pl
from jax.experimental.pallas import tpu as pltpu

F32 = jnp.float32
BF16 = jnp.bfloat16

V7X_VMEM_BYTES = 64 * 1024 * 1024
VMEM_LIMIT = V7X_VMEM_BYTES - 8 * 1024 * 1024
LANES = 128
SUBLANES = 8

BRANCH_W = 512
SG_CHUNK = 128
SG_GROUPS = 4
HG_HEADS = 4
HG_DK = BRANCH_W // HG_HEADS
HG_CHUNK = 16
HG_EPS = 1e-6
HG_EXP_CLIP = 60.0
RW_HEAD = 64
RW_HEADS = BRANCH_W // RW_HEAD
RW_CHUNK = 64
RW_W_LORA = 96
RW_A_LORA = 96
RW_G_LORA = 256
RW_GN_EPS = 64e-5
RW_LORA_PAD = 128
XA_HEADS = 4
XA_HEAD_DIM = 128
N_EXPERTS = 8
MOE_ROWS = 512
LN_EPS = 1e-5
DMA_UNROLL = 8
DMA_PRIORITIES = 2


def _params(*sem):
    return pltpu.CompilerParams(dimension_semantics=sem, vmem_limit_bytes=VMEM_LIMIT)


def _dot(a, b):
    return jnp.dot(a.astype(BF16), b.astype(BF16), preferred_element_type=F32)


def _dot_nt(a, b):
    return lax.dot_general(a.astype(BF16), b.astype(BF16), (((1,), (1,)), ((), ())),
                           preferred_element_type=F32)


def _dot_tn(a, b):
    return lax.dot_general(a.astype(BF16), b.astype(BF16), (((0,), (0,)), ((), ())),
                           preferred_element_type=F32)


def _split3(x):
    hi = x.astype(BF16)
    r1 = x - hi.astype(F32)
    mid = r1.astype(BF16)
    lo = (r1 - mid.astype(F32)).astype(BF16)
    return hi, mid, lo


def _dot_exact_lhs(m_bf16, x):
    hi, mid, lo = _split3(x)
    return (jnp.dot(m_bf16, hi, preferred_element_type=F32)
            + jnp.dot(m_bf16, mid, preferred_element_type=F32)
            + jnp.dot(m_bf16, lo, preferred_element_type=F32))


def _head_sum(x, ones_blk):
    rows = x.shape[0]
    n = x.shape[-1] // LANES
    parts = [p for i in range(n) for p in _split3(x[:, i * LANES:(i + 1) * LANES])]
    s = jnp.dot(jnp.concatenate(parts, axis=0), ones_blk, preferred_element_type=F32)
    tiles = [s[(3 * i) * rows:(3 * i + 1) * rows] + s[(3 * i + 1) * rows:(3 * i + 2) * rows]
             + s[(3 * i + 2) * rows:(3 * i + 3) * rows] for i in range(n)]
    return jnp.concatenate(tiles, axis=-1)


def _sigmoid(x):
    return 1.0 / (1.0 + jnp.exp(-x))


def _log_sigmoid(x):
    return jnp.minimum(x, 0.0) - jnp.log1p(jnp.exp(-jnp.abs(x)))


def _layer_norm(x, g, b, eps):
    mu = jnp.mean(x, axis=-1, keepdims=True)
    xc = x - mu
    var = jnp.mean(xc * xc, axis=-1, keepdims=True)
    return xc * lax.rsqrt(var + eps) * g + b


def _tril_mask(n, strict=False):
    row = lax.broadcasted_iota(jnp.int32, (n, n), 0)
    col = lax.broadcasted_iota(jnp.int32, (n, n), 1)
    return (row > col) if strict else (row >= col)


def _mm_kernel(a_ref, b_ref, o_ref):
    o_ref[...] = jnp.dot(a_ref[...], b_ref[...], preferred_element_type=F32).astype(o_ref.dtype)


def matmul(a, b, out_dtype, tm=1024, tn=512):
    m, k = a.shape
    n = b.shape[1]
    tm, tn = min(tm, m), min(tn, n)
    return pl.pallas_call(
        _mm_kernel,
        grid=(m // tm, n // tn),
        in_specs=[pl.BlockSpec((tm, k), lambda i, j: (i, 0)),
                  pl.BlockSpec((k, tn), lambda i, j: (0, j))],
        out_specs=pl.BlockSpec((tm, tn), lambda i, j: (i, j)),
        out_shape=jax.ShapeDtypeStruct((m, n), out_dtype),
        compiler_params=_params("parallel", "parallel"),
    )(a, b)


def _mm_res_ln_kernel(a_ref, w_ref, res_ref, g_ref, b_ref, of_ref, ob_ref, *, alpha):
    h = jnp.dot(a_ref[...], w_ref[...], preferred_element_type=F32)
    y = _layer_norm(alpha * res_ref[...] + h, g_ref[...], b_ref[...], LN_EPS)
    of_ref[...] = y
    ob_ref[...] = y.astype(BF16)


def matmul_res_ln(a, w, res, g, b, alpha, tm=512):
    m, k = a.shape
    n = w.shape[1]
    tm = min(tm, m)
    return pl.pallas_call(
        functools.partial(_mm_res_ln_kernel, alpha=alpha),
        grid=(m // tm,),
        in_specs=[pl.BlockSpec((tm, k), lambda i: (i, 0)),
                  pl.BlockSpec((k, n), lambda i: (0, 0)),
                  pl.BlockSpec((tm, n), lambda i: (i, 0)),
                  pl.BlockSpec((1, n), lambda i: (0, 0)),
                  pl.BlockSpec((1, n), lambda i: (0, 0))],
        out_specs=[pl.BlockSpec((tm, n), lambda i: (i, 0)),
                   pl.BlockSpec((tm, n), lambda i: (i, 0))],
        out_shape=[jax.ShapeDtypeStruct((m, n), F32), jax.ShapeDtypeStruct((m, n), BF16)],
        compiler_params=_params("parallel"),
    )(a, w, res, g.reshape(1, n), b.reshape(1, n))


def _sg_kernel(z_ref, w_ref, b_ref, g_ref, beta_ref, y_ref, *, n_chunks):
    z = z_ref[...]
    z = 0.5 * z * (1.0 + jnp.tanh(math.sqrt(2.0 / math.pi) * (z + 0.044715 * (z * z * z))))
    u = z[:, :BRANCH_W]
    v = _layer_norm(z[:, BRANCH_W:], g_ref[...], beta_ref[...], LN_EPS).astype(BF16)
    causal = _tril_mask(SG_CHUNK)
    gw = BRANCH_W // SG_GROUPS
    for g in range(SG_GROUPS):
        wg = jnp.where(causal, w_ref[g], 0.0).astype(BF16)
        for c in range(n_chunks):
            rows = slice(c * SG_CHUNK, (c + 1) * SG_CHUNK)
            cols = slice(g * gw, (g + 1) * gw)
            s = jnp.dot(wg, v[rows, cols], preferred_element_type=F32) + b_ref[g]
            y_ref[rows, cols] = (u[rows, cols] * s).astype(BF16)


def sg_branch(zall, col_block, sg_w, sg_b, ln_g, ln_b, tm=512):
    t = zall.shape[0]
    tm = min(tm, t)
    return pl.pallas_call(
        functools.partial(_sg_kernel, n_chunks=tm // SG_CHUNK),
        grid=(t // tm,),
        in_specs=[pl.BlockSpec((tm, 2 * BRANCH_W), lambda i: (i, col_block)),
                  pl.BlockSpec((SG_GROUPS, SG_CHUNK, SG_CHUNK), lambda i: (0, 0, 0)),
                  pl.BlockSpec((SG_GROUPS, SG_CHUNK, 1), lambda i: (0, 0, 0)),
                  pl.BlockSpec((1, BRANCH_W), lambda i: (0, 0)),
                  pl.BlockSpec((1, BRANCH_W), lambda i: (0, 0))],
        out_specs=pl.BlockSpec((tm, BRANCH_W), lambda i: (i, 0)),
        out_shape=jax.ShapeDtypeStruct((t, BRANCH_W), BF16),
        compiler_params=_params("parallel"),
    )(zall, sg_w, sg_b.reshape(SG_GROUPS, SG_CHUNK, 1), ln_g.reshape(1, -1), ln_b.reshape(1, -1))


def _hgrn_kernel(z_ref, lb_ref, ng_ref, y_ref, st_ref, *, batch):
    c = HG_CHUNK

    @pl.when(pl.program_id(0) == 0)
    def _():
        st_ref[...] = jnp.zeros_like(st_ref)

    lb = lb_ref[...]
    ltri = jnp.where(_tril_mask(c), 1.0, 0.0).astype(BF16)
    rows = lax.broadcasted_iota(jnp.int32, (c, 1), 0)
    w = BRANCH_W
    for b in range(batch):
        zq = z_ref[b, :, 0:w]
        zf = z_ref[b, :, w:2 * w]
        zi = z_ref[b, :, 2 * w:3 * w]
        zg = z_ref[b, :, 3 * w:4 * w]
        log_f = _log_sigmoid(zf) + jnp.log1p(lb * jnp.exp(jnp.minimum(-zf, HG_EXP_CLIP)))
        log_f = jnp.minimum(log_f, 0.0)
        kx = (1.0 - lb) * _sigmoid(-zf)
        q = zq * _sigmoid(zq)
        bc = _dot_exact_lhs(ltri, log_f)
        b_last = bc[c - 1:c, :]
        qd = q * jnp.exp(bc)
        kd = kx * jnp.exp(b_last - bc)
        e_last = jnp.exp(b_last)
        outs = []
        for h in range(HG_HEADS):
            sl = slice(h * HG_DK, (h + 1) * HG_DK)
            st = st_ref[b * HG_HEADS + h]
            o = _dot_nt(qd[:, sl], st)
            qh, kh, bh, vh = q[:, sl], kx[:, sl], bc[:, sl], zi[:, sl]
            parts = []
            for r0 in range(0, c, SUBLANES):
                rs = slice(r0, r0 + SUBLANES)
                qg, bg, og, rg = qh[rs], bh[rs], o[rs], rows[rs]
                for s in range(r0 + SUBLANES):
                    m = qg * kh[s:s + 1, :] * jnp.exp(jnp.minimum(bg - bh[s:s + 1, :], 0.0))
                    att = jnp.sum(m, axis=-1, keepdims=True)
                    if s > r0:
                        att = jnp.where(rg >= s, att, 0.0)
                    og = og + att * vh[s:s + 1, :]
                parts.append(og)
            o = jnp.concatenate(parts, axis=0)
            st_ref[b * HG_HEADS + h] = st * e_last[:, sl] + _dot_tn(vh, kd[:, sl])
            outs.append(o)
        o = jnp.concatenate(outs, axis=-1) * _sigmoid(zg)
        y = o * lax.rsqrt(jnp.mean(o * o, axis=-1, keepdims=True) + HG_EPS) * ng_ref[...]
        y_ref[b] = y.astype(BF16)


def hgrn2_branch(z3, col_block, lb, norm_g):
    batch, seq, _ = z3.shape
    return pl.pallas_call(
        functools.partial(_hgrn_kernel, batch=batch),
        grid=(seq // HG_CHUNK,),
        in_specs=[pl.BlockSpec((batch, HG_CHUNK, 4 * BRANCH_W), lambda c: (0, c, col_block)),
                  pl.BlockSpec((1, BRANCH_W), lambda c: (0, 0)),
                  pl.BlockSpec((1, BRANCH_W), lambda c: (0, 0))],
        out_specs=pl.BlockSpec((batch, HG_CHUNK, BRANCH_W), lambda c: (0, c, 0)),
        out_shape=jax.ShapeDtypeStruct((batch, seq, BRANCH_W), BF16),
        scratch_shapes=[pltpu.VMEM((batch * HG_HEADS, HG_DK, HG_DK), F32)],
        compiler_params=_params("arbitrary"),
    )(z3, lb.reshape(1, -1), norm_g.reshape(1, -1))


def _rw_prep_kernel(*refs, tiles_per_seq, has_vmix):
    if has_vmix:
        (z_ref, zp_ref, mu_ref, w0_ref, w2_ref, a0_ref, a2_ref, g2_ref, kk_ref, ka_ref, ones_ref,
         v0_ref, v1_ref, v2_ref, vf_ref, r_o, lw_o, k_o, v_o, kk_o, al_o, g_o) = refs
    else:
        (z_ref, zp_ref, mu_ref, w0_ref, w2_ref, a0_ref, a2_ref, g2_ref, kk_ref, ka_ref, ones_ref,
         r_o, lw_o, k_o, v_o, kk_o, al_o, g_o) = refs
    z = z_ref[...]
    tm = z.shape[0]
    first = (pl.program_id(0) % tiles_per_seq) == 0
    prev = jnp.where(first, 0.0, zp_ref[7:8, :])
    rows = lax.broadcasted_iota(jnp.int32, (tm, 1), 0)
    zs = jnp.where(rows == 0, prev, pltpu.roll(z, 1, 0))
    zc = z + (zs - z) * mu_ref[...]
    w = BRANCH_W
    p = RW_LORA_PAD
    r, k, v = zc[:, 0:w], zc[:, w:2 * w], zc[:, 2 * w:3 * w]
    wl, al_in, gl = zc[:, 3 * w:3 * w + p], zc[:, 3 * w + p:3 * w + 2 * p], zc[:, 3 * w + 2 * p:]
    u = w0_ref[...] + _dot(jnp.tanh(wl), w2_ref[...])
    lw = -math.exp(-0.5) * _sigmoid(u)
    a = _sigmoid(a0_ref[...] + _dot(al_in, a2_ref[...]))
    g = _dot(_sigmoid(gl), g2_ref[...])
    if has_vmix:
        mix = _sigmoid(v0_ref[...] + _dot(_dot(v, v1_ref[...]), v2_ref[...]))
        v = v + (vf_ref[...] - v) * mix
    kkx = k * kk_ref[...]
    norm = jnp.sqrt(_head_sum(kkx * kkx, ones_ref[...]))
    kkn = kkx / jnp.maximum(norm, 1e-12)
    r_o[...] = r
    lw_o[...] = lw
    k_o[...] = k * (1.0 + (a - 1.0) * ka_ref[...])
    v_o[...] = v
    kk_o[...] = kkn
    al_o[...] = a
    g_o[...] = g


def rw_prep(zall, col_block, seq, mu, w0, w2p, a0, a2p, g2, k_k, k_a, ones_blk, vmix, v_first, tm=256):
    t = zall.shape[0]
    tm = min(tm, seq)
    zw = 4 * BRANCH_W
    row = lambda i: (i, 0)
    const = lambda i: (0, 0)
    vec = pl.BlockSpec((1, BRANCH_W), const)
    in_specs = [pl.BlockSpec((tm, zw), lambda i: (i, col_block)),
                pl.BlockSpec((8, zw), lambda i: (jnp.maximum(i * (tm // 8) - 1, 0), col_block)),
                pl.BlockSpec((1, zw), const), vec,
                pl.BlockSpec((RW_LORA_PAD, BRANCH_W), const), vec,
                pl.BlockSpec((RW_LORA_PAD, BRANCH_W), const),
                pl.BlockSpec((RW_G_LORA, BRANCH_W), const), vec, vec,
                pl.BlockSpec((LANES, LANES), const)]
    args = [zall, zall, mu.reshape(1, -1), w0.reshape(1, -1), w2p, a0.reshape(1, -1), a2p, g2,
            k_k.reshape(1, -1), k_a.reshape(1, -1), ones_blk]
    if vmix is not None:
        v0, v1, v2 = vmix
        in_specs += [vec, pl.BlockSpec(v1.shape, const), pl.BlockSpec(v2.shape, const),
                     pl.BlockSpec((tm, BRANCH_W), row)]
        args += [v0.reshape(1, -1), v1, v2, v_first]
    out = jax.ShapeDtypeStruct((t, BRANCH_W), F32)
    return pl.pallas_call(
        functools.partial(_rw_prep_kernel, tiles_per_seq=seq // tm, has_vmix=vmix is not None),
        grid=(t // tm,),
        in_specs=in_specs,
        out_specs=[pl.BlockSpec((tm, BRANCH_W), row)] * 7,
        out_shape=[out] * 7,
        compiler_params=_params("parallel"),
    )(*args)


def _rwkv_kernel(r_ref, lw_ref, k_ref, v_ref, kk_ref, al_ref, g_ref, gng_ref, gnb_ref, rk_ref, ones_ref,
                 y_ref, st_ref, *, batch):
    c = RW_CHUNK

    @pl.when(pl.program_id(0) == 0)
    def _():
        st_ref[...] = jnp.zeros_like(st_ref)

    incl = _tril_mask(c)
    strict = _tril_mask(c, strict=True)
    ltri = jnp.where(incl, 1.0, 0.0).astype(BF16)
    units = [(b, h) for b in range(batch) for h in range(RW_HEADS)]
    ar, bk, vv, st, g_last = {}, {}, {}, {}, {}
    for b in range(batch):
        lw, kk = lw_ref[b], kk_ref[b]
        cum = _dot_exact_lhs(ltri, lw)
        gam = jnp.exp(cum)
        ginv = jnp.exp(-cum)
        at = -kk * jnp.exp(cum - lw)
        rt = r_ref[b] * gam
        bt = kk * al_ref[b] * ginv
        kt = k_ref[b] * ginv
        v = v_ref[b]
        for h in range(RW_HEADS):
            sl = slice(h * RW_HEAD, (h + 1) * RW_HEAD)
            ar[b, h] = jnp.concatenate([at[:, sl], rt[:, sl]], axis=0).astype(BF16)
            bk[b, h] = jnp.concatenate([bt[:, sl], kt[:, sl]], axis=0).astype(BF16)
            vv[b, h] = v[:, sl]
            st[b, h] = st_ref[b * RW_HEADS + h]
            g_last[b, h] = gam[c - 1:c, sl]
    nb = {u: _dot_nt(ar[u], bk[u][:c]) for u in units}
    nk = {u: _dot_nt(ar[u], bk[u][c:]) for u in units}
    a_s = {u: _dot_nt(ar[u], st[u]) for u in units}
    n = {u: jnp.where(strict, nb[u][:c], 0.0) for u in units}
    rb = {u: jnp.where(incl, nb[u][c:], 0.0) for u in units}
    akrk = {u: jnp.concatenate([jnp.where(strict, nk[u][:c], 0.0), jnp.where(incl, nk[u][c:], 0.0)], axis=0)
            for u in units}
    kv = {u: _dot(akrk[u], vv[u]) for u in units}
    x = {u: a_s[u][:c] + kv[u][:c] for u in units}
    right = lax.broadcasted_iota(jnp.int32, (c, 2 * c), 1) >= c
    w = {u: jnp.concatenate([n[u], x[u]], axis=1) for u in units}
    for _ in range(int(math.log2(c))):
        w = {u: jnp.where(right, w[u], 0.0) + _dot(w[u][:, :c], w[u]) for u in units}
    uu = {u: w[u][:, c:] for u in units}
    y = {u: a_s[u][c:] + kv[u][c:] + _dot(rb[u], uu[u]) for u in units}
    for u in units:
        upd = _dot_tn(jnp.concatenate([uu[u], vv[u]], axis=0), bk[u])
        st_ref[u[0] * RW_HEADS + u[1]] = (st[u] + upd) * g_last[u]
    ones = ones_ref[...]
    inv_n = 1.0 / RW_HEAD
    for b in range(batch):
        yb = jnp.concatenate([y[b, h] for h in range(RW_HEADS)], axis=-1)
        mu_y = _head_sum(yb, ones) * inv_n
        yc = yb - mu_y
        var = _head_sum(yc * yc, ones) * inv_n
        yn = yc * lax.rsqrt(var + RW_GN_EPS) * gng_ref[...] + gnb_ref[...]
        bonus = _head_sum(r_ref[b] * k_ref[b] * rk_ref[...], ones)
        y_ref[b] = ((yn + bonus * v_ref[b]) * g_ref[b]).astype(BF16)


def rwkv7_mix(r, lw, k, v, kk, al, g, gn_g, gn_b, r_k, ones_blk, batch, seq):
    arrs = [x.reshape(batch, seq, BRANCH_W) for x in (r, lw, k, v, kk, al, g)]
    blk = pl.BlockSpec((batch, RW_CHUNK, BRANCH_W), lambda c: (0, c, 0))
    vec = pl.BlockSpec((1, BRANCH_W), lambda c: (0, 0))
    return pl.pallas_call(
        functools.partial(_rwkv_kernel, batch=batch),
        grid=(seq // RW_CHUNK,),
        in_specs=[blk] * 7 + [vec] * 3 + [pl.BlockSpec((LANES, LANES), lambda c: (0, 0))],
        out_specs=blk,
        out_shape=jax.ShapeDtypeStruct((batch, seq, BRANCH_W), BF16),
        scratch_shapes=[pltpu.VMEM((batch * RW_HEADS, RW_HEAD, RW_HEAD), F32)],
        compiler_params=_params("arbitrary"),
    )(*arrs, gn_g.reshape(1, -1), gn_b.reshape(1, -1), r_k.reshape(1, -1), ones_blk)


def _merge_kernel(x_ref, ysg_ref, yhg_ref, yrw_ref, wg_ref, wb_ref, o_ref):
    x = x_ref[...]
    acc = None
    for b, y_ref in enumerate((ysg_ref, yhg_ref, yrw_ref)):
        gate = _sigmoid(jnp.dot(x, wg_ref[b], preferred_element_type=F32))
        term = gate * jnp.dot(y_ref[...], wb_ref[b], preferred_element_type=F32)
        acc = term if acc is None else acc + term
    o_ref[...] = acc.astype(BF16)


def merge_branches(xb, y_sg, y_hg, y_rw, w_gate, w_branch, tm=1024, tn=512):
    t, d = xb.shape
    tm = min(tm, t)
    ysp = pl.BlockSpec((tm, BRANCH_W), lambda i, j: (i, 0))
    return pl.pallas_call(
        _merge_kernel,
        grid=(t // tm, d // tn),
        in_specs=[pl.BlockSpec((tm, d), lambda i, j: (i, 0)), ysp, ysp, ysp,
                  pl.BlockSpec((3, d, tn), lambda i, j: (0, 0, j)),
                  pl.BlockSpec((3, BRANCH_W, tn), lambda i, j: (0, 0, j))],
        out_specs=pl.BlockSpec((tm, tn), lambda i, j: (i, j)),
        out_shape=jax.ShapeDtypeStruct((t, d), BF16),
        compiler_params=_params("parallel", "parallel"),
    )(xb, y_sg, y_hg, y_rw, w_gate, w_branch)


def _xattn_kernel(xb_ref, xf_ref, k_ref, v_ref, wq_ref, wo_ref, g_ref, b_ref, of_ref, ob_ref, *, alpha):
    q = jnp.dot(xb_ref[0], wq_ref[...], preferred_element_type=F32)
    kmem, vmem = k_ref[0], v_ref[0]
    scale = XA_HEAD_DIM ** -0.5
    outs = []
    for h in range(XA_HEADS):
        sl = slice(h * XA_HEAD_DIM, (h + 1) * XA_HEAD_DIM)
        s = _dot_nt(q[:, sl], kmem[:, sl]) * scale
        e = jnp.exp(s - jnp.max(s, axis=-1, keepdims=True))
        p = e / jnp.sum(e, axis=-1, keepdims=True)
        outs.append(_dot(p, vmem[:, sl]))
    o = jnp.concatenate(outs, axis=-1)
    h_out = _dot(o, wo_ref[...])
    y = _layer_norm(alpha * xf_ref[0] + h_out, g_ref[...], b_ref[...], LN_EPS)
    of_ref[0] = y
    ob_ref[0] = y.astype(BF16)


def cross_attention_ln(xb, xf, kmem, vmem, wq, wo, g, b, alpha, tm=512):
    batch, seq, d = xf.shape
    m, xw = kmem.shape[1], kmem.shape[2]
    tm = min(tm, seq)
    xs = pl.BlockSpec((1, tm, d), lambda bi, i: (bi, i, 0))
    ms = pl.BlockSpec((1, m, xw), lambda bi, i: (bi, 0, 0))
    vec = pl.BlockSpec((1, d), lambda bi, i: (0, 0))
    return pl.pallas_call(
        functools.partial(_xattn_kernel, alpha=alpha),
        grid=(batch, seq // tm),
        in_specs=[xs, xs, ms, ms,
                  pl.BlockSpec((d, xw), lambda bi, i: (0, 0)),
                  pl.BlockSpec((xw, d), lambda bi, i: (0, 0)), vec, vec],
        out_specs=[xs, xs],
        out_shape=[jax.ShapeDtypeStruct((batch, seq, d), F32), jax.ShapeDtypeStruct((batch, seq, d), BF16)],
        compiler_params=_params("parallel", "parallel"),
    )(xb, xf, kmem, vmem, wq, wo, g.reshape(1, d), b.reshape(1, d))


def _ffn_kernel(be_ref, nu_ref, *refs, alpha, final_ln, gather, nf):
    del be_ref
    if gather:
        idx0_ref, idxn_ref, x_ref, w1_ref, w3_ref, w2_ref, g_ref, b_ref, of_ref, xb_scr, acc_scr, xg_scr, sems = refs
    elif final_ln:
        x_ref, w1_ref, w3_ref, w2_ref, g_ref, b_ref, of_ref, ob_ref, xb_scr, acc_scr = refs
    else:
        x_ref, w1_ref, w3_ref, w2_ref, g_ref, b_ref, of_ref, xb_scr, acc_scr = refs
    i = pl.program_id(0)
    f = pl.program_id(1)
    last = f == nf - 1
    n_used = nu_ref[0]
    active = i < n_used
    tm = xb_scr.shape[0]
    per_step = -(-tm // nf)
    slot = i % 2

    if gather:
        @pl.when((i == 0) & (f == 0))
        def _():
            _gather_rows(idx0_ref, x_ref, xg_scr.at[0], sems.at[0], tm)

        @pl.when((f == 0) & (i >= 1) & (i <= n_used))
        def _():
            def wait(r, carry):
                _row_copy(x_ref, xg_scr.at[slot], sems.at[slot], 0, r).wait()
                return carry
            lax.fori_loop(0, per_step * nf, wait, 0, unroll=DMA_UNROLL)

    @pl.when(active & (f == 0))
    def _():
        if gather:
            xb_scr[...] = xg_scr[slot, 0:tm, :].astype(BF16)
        else:
            xb_scr[...] = x_ref[...].astype(BF16)
        acc_scr[...] = jnp.zeros_like(acc_scr)

    @pl.when(active)
    def _():
        if gather:
            for j in range(per_step):
                r = f * per_step + j
                src_row = idxn_ref[0, 0, jnp.minimum(r, tm - 1)]
                _row_copy(x_ref, xg_scr.at[1 - slot], sems.at[1 - slot], src_row, r).start(priority=1)
        xb = xb_scr[...]
        gate = jnp.dot(xb, w1_ref[0], preferred_element_type=F32)
        up = jnp.dot(xb, w3_ref[0], preferred_element_type=F32)
        hh = (gate * _sigmoid(gate) * up).astype(BF16)
        acc_scr[...] += jnp.dot(hh, w2_ref[0], preferred_element_type=F32)

    @pl.when(active & last)
    def _():
        if final_ln:
            y = _layer_norm(alpha * x_ref[...] + acc_scr[...], g_ref[...], b_ref[...], LN_EPS)
            of_ref[...] = y
            ob_ref[...] = y.astype(BF16)
        else:
            of_ref[...] = acc_scr[...]

    @pl.when(jnp.logical_not(active) & last)
    def _():
        of_ref[...] = jnp.zeros_like(of_ref)
        if final_ln:
            ob_ref[...] = jnp.zeros_like(ob_ref)


def swiglu(x, block_expert, n_used, w1, w3, w2, g, b, alpha, final_ln, slot_rows=None, tm=512, tf=512):
    gather = slot_rows is not None
    assert not (gather and final_ln)
    d = x.shape[1]
    rows = slot_rows.shape[0] if gather else x.shape[0]
    ff = w1.shape[2]
    tm = min(tm, rows)
    nf = ff // tf
    n_blocks = rows // tm
    fi = lambda i, f, nu: jnp.where(i < nu[0], f, nf - 1)
    xs = pl.BlockSpec((tm, d), lambda i, f, be, nu: (i, 0))
    vec = pl.BlockSpec((1, d), lambda i, f, be, nu: (0, 0))
    weights = [pl.BlockSpec((1, d, tf), lambda i, f, be, nu: (be[i], 0, fi(i, f, nu))),
               pl.BlockSpec((1, d, tf), lambda i, f, be, nu: (be[i], 0, fi(i, f, nu))),
               pl.BlockSpec((1, tf, d), lambda i, f, be, nu: (be[i], fi(i, f, nu), 0)),
               vec, vec]
    scratch = [pltpu.VMEM((tm, d), BF16), pltpu.VMEM((tm, d), F32)]
    if gather:
        idx = slot_rows.reshape(n_blocks, 1, tm)
        smem_rows = lambda index_map: pl.BlockSpec((1, 1, tm), index_map, memory_space=pltpu.SMEM)
        in_specs = [smem_rows(lambda i, f, be, nu: (0, 0, 0)),
                    smem_rows(lambda i, f, be, nu: (jnp.minimum(i + 1, n_blocks - 1), 0, 0)),
                    pl.BlockSpec(memory_space=pl.ANY)] + weights
        args = (idx, idx, x)
        fetch_rows = -(-tm // nf) * nf
        scratch += [pltpu.VMEM((2, -(-fetch_rows // SUBLANES) * SUBLANES, d), F32), pltpu.SemaphoreType.DMA((2,))]
    else:
        in_specs = [pl.BlockSpec((tm, d), lambda i, f, be, nu: (jnp.minimum(i, nu[0] - 1), 0))] + weights
        args = (x,)
    n_out = 2 if final_ln else 1
    out_shape = [jax.ShapeDtypeStruct((rows, d), F32), jax.ShapeDtypeStruct((rows, d), BF16)][:n_out]
    res = pl.pallas_call(
        functools.partial(_ffn_kernel, alpha=alpha, final_ln=final_ln, gather=gather, nf=nf),
        grid_spec=pltpu.PrefetchScalarGridSpec(
            num_scalar_prefetch=2,
            grid=(n_blocks, nf),
            in_specs=in_specs,
            out_specs=[xs] * n_out,
            scratch_shapes=scratch),
        out_shape=out_shape,
        compiler_params=_params("arbitrary" if gather else "parallel", "arbitrary"),
    )(block_expert, n_used, *args, w1, w3, w2, g.reshape(1, d), b.reshape(1, d))
    return res if final_ln else res[0]


def _router_kernel(x_ref, w_ref, o_ref):
    x = x_ref[...]
    hi, mid, _ = _split3(x)
    whi, wmid, _ = _split3(w_ref[...])
    dot = lambda a, b: jnp.dot(a, b, preferred_element_type=F32)
    logits = dot(hi, whi) + (dot(hi, wmid) + dot(mid, whi))
    lane = lax.broadcasted_iota(jnp.int32, logits.shape, 1)
    neg = -jnp.inf
    lg = jnp.where(lane < N_EXPERTS, logits, neg)
    m1 = jnp.max(lg, axis=-1, keepdims=True)
    i1 = jnp.min(jnp.where(lg == m1, lane, LANES), axis=-1, keepdims=True)
    lg2 = jnp.where(lane == i1, neg, lg)
    m2 = jnp.max(lg2, axis=-1, keepdims=True)
    i2 = jnp.min(jnp.where(lg2 == m2, lane, LANES), axis=-1, keepdims=True)
    e = jnp.exp(m2 - m1)
    g1 = 1.0 / (1.0 + e)
    g2 = e / (1.0 + e)
    out = jnp.where(lane == 0, i1.astype(F32),
                    jnp.where(lane == 1, i2.astype(F32),
                              jnp.where(lane == 2, g1, jnp.where(lane == 3, g2, 0.0))))
    o_ref[...] = out


def route_top2(x, w_router_pad, tm=1024):
    t, d = x.shape
    tm = min(tm, t)
    return pl.pallas_call(
        _router_kernel,
        grid=(t // tm,),
        in_specs=[pl.BlockSpec((tm, d), lambda i: (i, 0)), pl.BlockSpec((d, LANES), lambda i: (0, 0))],
        out_specs=pl.BlockSpec((tm, LANES), lambda i: (i, 0)),
        out_shape=jax.ShapeDtypeStruct((t, LANES), F32),
        compiler_params=_params("parallel"),
    )(x, w_router_pad)


def _row_copy(src_hbm, dst_ref, sem, src_row, dst_row):
    return pltpu.make_async_copy(src_hbm.at[pl.ds(src_row, 1)], dst_ref.at[pl.ds(dst_row, 1)], sem)


def _gather_rows(idx_ref, src_hbm, dst_ref, sem, rows):
    def start(i, carry):
        for p in range(DMA_PRIORITIES):
            r = i * DMA_PRIORITIES + p
            _row_copy(src_hbm, dst_ref, sem, idx_ref[0, 0, r], r).start(priority=p)
        return carry

    def wait(r, carry):
        _row_copy(src_hbm, dst_ref, sem, 0, r).wait()
        return carry

    lax.fori_loop(0, rows // DMA_PRIORITIES, start, 0, unroll=DMA_UNROLL // DMA_PRIORITIES)
    lax.fori_loop(0, rows, wait, 0, unroll=DMA_UNROLL)


def _combine_kernel(idx0_ref, idxn_ref, y_hbm, x_ref, gate_ref, g_ref, b_ref, of_ref, ob_ref, ybuf, sems, *, alpha):
    tm = x_ref.shape[0]
    i = pl.program_id(0)
    slot = i % 2

    def wait_rows(s):
        def wait(r, carry):
            _row_copy(y_hbm, ybuf.at[s], sems.at[s], 0, r).wait()
            return carry
        lax.fori_loop(0, 2 * tm, wait, 0, unroll=DMA_UNROLL)

    @pl.when(i == 0)
    def _():
        _gather_rows(idx0_ref, y_hbm, ybuf.at[0], sems.at[0], 2 * tm)

    @pl.when(i >= 1)
    def _():
        wait_rows(slot)

    for r in range(2 * tm):
        _row_copy(y_hbm, ybuf.at[1 - slot], sems.at[1 - slot], idxn_ref[0, 0, r], r).start(
            priority=r % DMA_PRIORITIES)
    gates = gate_ref[...]
    h = gates[:, 0:1] * ybuf[slot, 0:tm, :] + gates[:, 1:2] * ybuf[slot, tm:2 * tm, :]
    y = _layer_norm(alpha * x_ref[...] + h, g_ref[...], b_ref[...], LN_EPS)
    of_ref[...] = y
    ob_ref[...] = y.astype(BF16)

    @pl.when(i == pl.num_programs(0) - 1)
    def _():
        wait_rows(1 - slot)


def moe_combine_ln(y_slots, dest, gates, x, g, b, alpha, tm=256):
    t, d = x.shape
    tm = min(tm, t)
    n_tiles = t // tm
    idx = dest.reshape(n_tiles, tm, 2).transpose(0, 2, 1).reshape(n_tiles, 1, 2 * tm)
    row = lambda i: (i, 0)
    vec = pl.BlockSpec((1, d), lambda i: (0, 0))
    smem_rows = lambda index_map: pl.BlockSpec((1, 1, 2 * tm), index_map, memory_space=pltpu.SMEM)
    return pl.pallas_call(
        functools.partial(_combine_kernel, alpha=alpha),
        grid=(n_tiles,),
        in_specs=[smem_rows(lambda i: (0, 0, 0)),
                  smem_rows(lambda i: (jnp.minimum(i + 1, n_tiles - 1), 0, 0)),
                  pl.BlockSpec(memory_space=pl.ANY),
                  pl.BlockSpec((tm, d), row), pl.BlockSpec((tm, 2), row), vec, vec],
        out_specs=[pl.BlockSpec((tm, d), row), pl.BlockSpec((tm, d), row)],
        out_shape=[jax.ShapeDtypeStruct((t, d), F32), jax.ShapeDtypeStruct((t, d), BF16)],
        scratch_shapes=[pltpu.VMEM((2, 2 * tm, d), F32), pltpu.SemaphoreType.DMA((2,))],
        compiler_params=_params("arbitrary"),
    )(idx, idx, y_slots, x, gates, g.reshape(1, d), b.reshape(1, d))


def moe_swiglu_ln(xf, w_router_pad, w1, w3, w2, g, b, alpha):
    t, d = xf.shape
    routed = route_top2(xf, w_router_pad)
    expert = routed[:, 0:2].astype(jnp.int32)
    gates = routed[:, 2:4]
    flat_e = expert.reshape(-1)
    onehot = (flat_e[:, None] == jnp.arange(N_EXPERTS, dtype=jnp.int32)[None, :]).astype(jnp.int32)
    rank = jnp.sum((jnp.cumsum(onehot, axis=0) - 1) * onehot, axis=1)
    counts = jnp.sum(onehot, axis=0)
    padded = (counts + MOE_ROWS - 1) // MOE_ROWS * MOE_ROWS
    ends = jnp.cumsum(padded)
    dest = (ends - padded)[flat_e] + rank
    n_blocks = (t * 2) // MOE_ROWS + N_EXPERTS
    n_slots = n_blocks * MOE_ROWS
    flat_tok = jnp.arange(t * 2, dtype=jnp.int32) // 2
    slot_tok = jnp.zeros((n_slots,), jnp.int32).at[dest].set(flat_tok)
    block_start = jnp.arange(n_blocks, dtype=jnp.int32) * MOE_ROWS
    block_expert = jnp.minimum(jnp.sum((ends[None, :] <= block_start[:, None]).astype(jnp.int32), axis=1),
                               N_EXPERTS - 1)
    n_used = (ends[N_EXPERTS - 1:] // MOE_ROWS).astype(jnp.int32)
    y_slots = swiglu(xf, block_expert, n_used, w1, w3, w2, g, b, alpha, final_ln=False, slot_rows=slot_tok,
                     tm=MOE_ROWS)
    return moe_combine_ln(y_slots, dest.reshape(t, 2).astype(jnp.int32), gates, xf, g, b, alpha)


def kernel(x, mem, w_in, sg_w, sg_b, sg_ln_g, sg_ln_b, hg_lb_logits, hg_norm_g, rw_mu, rw_w0, rw_w2, rw_a0, rw_a2, rw_g2, rw_k_k, rw_k_a, rw_r_k, rw_gn_g, rw_gn_b, rw_v0, rw_v1, rw_v2, w_branch, w_mix_out, xa_wq, xa_wk, xa_wv, xa_wo, ln_g, ln_b, ffn_w1, ffn_w3, ffn_w2, moe_router, moe_w1, moe_w3, moe_w2):
    batch, seq, d = x.shape
    depth = w_in.shape[0]
    t = batch * seq
    alpha = (2 * depth) ** 0.25
    w = BRANCH_W
    sg_cols, hg_cols = 2 * w, 4 * w
    rw_cols = 3 * w + RW_W_LORA + RW_A_LORA + RW_G_LORA
    o_hg, o_rw, o_gate = sg_cols, sg_cols + hg_cols, sg_cols + hg_cols + rw_cols

    p = jax.nn.softmax(hg_lb_logits.astype(F32), axis=0)
    lower_bounds = jnp.cumsum(p, axis=0) - p[0]
    lane_head = jnp.arange(LANES) // RW_HEAD
    ones_blk = (lane_head[:, None] == lane_head[None, :]).astype(BF16)

    def pad_cols(m, n):
        return jnp.pad(m, ((0, 0), (0, n - m.shape[1])))

    def pad_rows(m, n):
        return jnp.pad(m, ((0, n - m.shape[0]), (0, 0)))

    mem_b = mem.reshape(-1, d).astype(BF16)
    xf = x.reshape(t, d)
    xb = xf.astype(BF16)
    v_first = None
    for l in range(depth):
        wl_ = w_in[l]
        c0 = o_rw + 3 * w
        w_rw = jnp.concatenate([wl_[:, o_rw:c0],
                                pad_cols(wl_[:, c0:c0 + RW_W_LORA], RW_LORA_PAD),
                                pad_cols(wl_[:, c0 + RW_W_LORA:c0 + RW_W_LORA + RW_A_LORA], RW_LORA_PAD),
                                wl_[:, c0 + RW_W_LORA + RW_A_LORA:o_gate]], axis=1)
        w_proj = jnp.concatenate([wl_[:, o_hg:o_rw], w_rw, wl_[:, :o_hg]], axis=1).astype(BF16)
        w_gate = wl_[:, o_gate:].reshape(d, 3, d).transpose(1, 0, 2).astype(BF16)
        mu = rw_mu[l]
        mu_p = jnp.concatenate([mu[:3 * w],
                                jnp.pad(mu[3 * w:3 * w + RW_W_LORA], (0, RW_LORA_PAD - RW_W_LORA)),
                                jnp.pad(mu[3 * w + RW_W_LORA:3 * w + RW_W_LORA + RW_A_LORA],
                                        (0, RW_LORA_PAD - RW_A_LORA)),
                                mu[3 * w + RW_W_LORA + RW_A_LORA:]])

        zall = matmul(xb, w_proj, F32, tm=2048, tn=1024)
        y_sg = sg_branch(zall, 4, sg_w[l], sg_b[l], sg_ln_g[l], sg_ln_b[l])
        y_hg = hgrn2_branch(zall.reshape(batch, seq, -1), 0, lower_bounds[l], hg_norm_g[l]).reshape(t, w)
        vmix = None if l == 0 else (rw_v0[l - 1], rw_v1[l - 1].astype(BF16), rw_v2[l - 1].astype(BF16))
        r_, lw_, k_, v_, kk_, al_, g_ = rw_prep(
            zall, 1, seq, mu_p, rw_w0[l], pad_rows(rw_w2[l], RW_LORA_PAD).astype(BF16), rw_a0[l],
            pad_rows(rw_a2[l], RW_LORA_PAD).astype(BF16), rw_g2[l].astype(BF16), rw_k_k[l], rw_k_a[l],
            ones_blk, vmix, v_first)
        if l == 0:
            v_first = v_
        y_rw = rwkv7_mix(r_, lw_, k_, v_, kk_, al_, g_, rw_gn_g[l], rw_gn_b[l], rw_r_k[l], ones_blk,
                         batch, seq).reshape(t, w)
        merged = merge_branches(xb, y_sg, y_hg, y_rw, w_gate, w_branch[l].astype(BF16))
        xf, xb = matmul_res_ln(merged, w_mix_out[l].astype(BF16), xf, ln_g[l, 0], ln_b[l, 0], alpha)

        kmem = matmul(mem_b, xa_wk[l].astype(BF16), BF16).reshape(batch, -1, XA_HEADS * XA_HEAD_DIM)
        vmem = matmul(mem_b, xa_wv[l].astype(BF16), BF16).reshape(batch, -1, XA_HEADS * XA_HEAD_DIM)
        xf3, xb3 = cross_attention_ln(xb.reshape(batch, seq, d), xf.reshape(batch, seq, d), kmem, vmem,
                                      xa_wq[l].astype(BF16), xa_wo[l].astype(BF16), ln_g[l, 1], ln_b[l, 1], alpha)
        xf, xb = xf3.reshape(t, d), xb3.reshape(t, d)

        if l % 2 == 0:
            i = l // 2
            n_row_blocks = t // min(512, t)
            xf, xb = swiglu(xf, jnp.zeros((n_row_blocks,), jnp.int32), jnp.full((1,), n_row_blocks, jnp.int32),
                            ffn_w1[i:i + 1].astype(BF16),
                            ffn_w3[i:i + 1].astype(BF16), ffn_w2[i:i + 1].astype(BF16),
                            ln_g[l, 2], ln_b[l, 2], alpha, final_ln=True)
        else:
            i = l // 2
            xf, xb = moe_swiglu_ln(xf, pad_cols(moe_router[i], LANES), moe_w1[i].astype(BF16),
                                   moe_w3[i].astype(BF16), moe_w2[i].astype(BF16), ln_g[l, 2], ln_b[l, 2], alpha)
    return xf.reshape(batch, seq, d)
```

```python
import functools
import math

import jax
import jax.numpy as jnp
from jax import lax
from jax.experimental import pallas as pl
from jax.experimental.pallas import tpu as pltpu

F32 = jnp.float32
BF16 = jnp.bfloat16

V7X_VMEM_BYTES = 64 * 1024 * 1024
VMEM_LIMIT = V7X_VMEM_BYTES - 8 * 1024 * 1024
LANES = 128
SUBLANES = 8

BRANCH_W = 512
SG_CHUNK = 128
SG_GROUPS = 4
HG_HEADS = 4
HG_DK = BRANCH_W // HG_HEADS
HG_CHUNK = 16
HG_EPS = 1e-6
HG_EXP_CLIP = 60.0
RW_HEAD = 64
RW_HEADS = BRANCH_W // RW_HEAD
RW_CHUNK = 64
RW_W_LORA = 96
RW_A_LORA = 96
RW_G_LORA = 256
RW_GN_EPS = 64e-5
RW_LORA_PAD = 128
XA_HEADS = 4
XA_HEAD_DIM = 128
N_EXPERTS = 8
MOE_ROWS = 512
LN_EPS = 1e-5
DMA_UNROLL = 8
DMA_PRIORITIES = 2


def _params(*sem):
    return pltpu.CompilerParams(dimension_semantics=sem, vmem_limit_bytes=VMEM_LIMIT)


def _dot(a, b):
    return jnp.dot(a.astype(BF16), b.astype(BF16), preferred_element_type=F32)


def _dot_nt(a, b):
    return lax.dot_general(a.astype(BF16), b.astype(BF16), (((1,), (1,)), ((), ())),
                           preferred_element_type=F32)


def _dot_tn(a, b):
    return lax.dot_general(a.astype(BF16), b.astype(BF16), (((0,), (0,)), ((), ())),
                           preferred_element_type=F32)


def _split3(x):
    hi = x.astype(BF16)
    r1 = x - hi.astype(F32)
    mid = r1.astype(BF16)
    lo = (r1 - mid.astype(F32)).astype(BF16)
    return hi, mid, lo


def _dot_exact_lhs(m_bf16, x):
    hi, mid, lo = _split3(x)
    return (jnp.dot(m_bf16, hi, preferred_element_type=F32)
            + jnp.dot(m_bf16, mid, preferred_element_type=F32)
            + jnp.dot(m_bf16, lo, preferred_element_type=F32))


def _head_sums(xs, ones_blk):
    rows = xs[0].shape[0]
    n = xs[0].shape[-1] // LANES
    parts = [p for x in xs for i in range(n) for p in _split3(x[:, i * LANES:(i + 1) * LANES])]
    s = jnp.dot(jnp.concatenate(parts, axis=0), ones_blk, preferred_element_type=F32)
    piece = lambda j: s[j * rows:(j + 1) * rows]
    return [jnp.concatenate([piece(3 * (a * n + i)) + piece(3 * (a * n + i) + 1) + piece(3 * (a * n + i) + 2)
                             for i in range(n)], axis=-1) for a in range(len(xs))]


def _head_sum(x, ones_blk):
    return _head_sums([x], ones_blk)[0]


def _sigmoid(x):
    return 1.0 / (1.0 + jnp.exp(-x))


def _log_sigmoid(x):
    return jnp.minimum(x, 0.0) - jnp.log(1.0 + jnp.exp(-jnp.abs(x)))


def _layer_norm(x, g, b, eps):
    mu = jnp.mean(x, axis=-1, keepdims=True)
    xc = x - mu
    var = jnp.mean(xc * xc, axis=-1, keepdims=True)
    return xc * lax.rsqrt(var + eps) * g + b


def _tril_mask(n, strict=False):
    row = lax.broadcasted_iota(jnp.int32, (n, n), 0)
    col = lax.broadcasted_iota(jnp.int32, (n, n), 1)
    return (row > col) if strict else (row >= col)


def _mm_kernel(a_ref, b_ref, o_ref):
    o_ref[...] = jnp.dot(a_ref[...], b_ref[...], preferred_element_type=F32).astype(o_ref.dtype)


def matmul(a, b, out_dtype, tm=1024, tn=512):
    m, k = a.shape
    n = b.shape[1]
    tm, tn = min(tm, m), min(tn, n)
    return pl.pallas_call(
        _mm_kernel,
        grid=(m // tm, n // tn),
        in_specs=[pl.BlockSpec((tm, k), lambda i, j: (i, 0)),
                  pl.BlockSpec((k, tn), lambda i, j: (0, j))],
        out_specs=pl.BlockSpec((tm, tn), lambda i, j: (i, j)),
        out_shape=jax.ShapeDtypeStruct((m, n), out_dtype),
        compiler_params=_params("parallel", "parallel"),
    )(a, b)


def _mm_res_ln_kernel(a_ref, w_ref, res_ref, g_ref, b_ref, of_ref, ob_ref, *, alpha):
    half = a_ref.shape[0] // 2
    for r in range(2):
        rs = slice(r * half, (r + 1) * half)
        h = jnp.dot(a_ref[rs, :], w_ref[...], preferred_element_type=F32)
        y = _layer_norm(alpha * res_ref[rs, :] + h, g_ref[...], b_ref[...], LN_EPS)
        of_ref[rs, :] = y
        ob_ref[rs, :] = y.astype(BF16)


def matmul_res_ln(a, w, res, g, b, alpha, tm=512):
    m, k = a.shape
    n = w.shape[1]
    tm = min(tm, m)
    return pl.pallas_call(
        functools.partial(_mm_res_ln_kernel, alpha=alpha),
        grid=(m // tm,),
        in_specs=[pl.BlockSpec((tm, k), lambda i: (i, 0)),
                  pl.BlockSpec((k, n), lambda i: (0, 0)),
                  pl.BlockSpec((tm, n), lambda i: (i, 0)),
                  pl.BlockSpec((1, n), lambda i: (0, 0)),
                  pl.BlockSpec((1, n), lambda i: (0, 0))],
        out_specs=[pl.BlockSpec((tm, n), lambda i: (i, 0)),
                   pl.BlockSpec((tm, n), lambda i: (i, 0))],
        out_shape=[jax.ShapeDtypeStruct((m, n), F32), jax.ShapeDtypeStruct((m, n), BF16)],
        compiler_params=_params("parallel"),
    )(a, w, res, g.reshape(1, n), b.reshape(1, n))


def _sg_kernel(z_ref, w_ref, b_ref, g_ref, beta_ref, y_ref, *, n_chunks):
    z = z_ref[...]
    z = 0.5 * z * (1.0 + jnp.tanh(math.sqrt(2.0 / math.pi) * (z + 0.044715 * (z * z * z))))
    u = z[:, :BRANCH_W]
    v = _layer_norm(z[:, BRANCH_W:], g_ref[...], beta_ref[...], LN_EPS).astype(BF16)
    causal = _tril_mask(SG_CHUNK)
    gw = BRANCH_W // SG_GROUPS
    for g in range(SG_GROUPS):
        wg = jnp.where(causal, w_ref[g], 0.0).astype(BF16)
        for c in range(n_chunks):
            rows = slice(c * SG_CHUNK, (c + 1) * SG_CHUNK)
            cols = slice(g * gw, (g + 1) * gw)
            s = jnp.dot(wg, v[rows, cols], preferred_element_type=F32) + b_ref[g]
            y_ref[rows, cols] = (u[rows, cols] * s).astype(BF16)


def sg_branch(zall, col_block, sg_w, sg_b, ln_g, ln_b, tm=512):
    t = zall.shape[0]
    tm = min(tm, t)
    return pl.pallas_call(
        functools.partial(_sg_kernel, n_chunks=tm // SG_CHUNK),
        grid=(t // tm,),
        in_specs=[pl.BlockSpec((tm, 2 * BRANCH_W), lambda i: (i, col_block)),
                  pl.BlockSpec((SG_GROUPS, SG_CHUNK, SG_CHUNK), lambda i: (0, 0, 0)),
                  pl.BlockSpec((SG_GROUPS, SG_CHUNK, 1), lambda i: (0, 0, 0)),
                  pl.BlockSpec((1, BRANCH_W), lambda i: (0, 0)),
                  pl.BlockSpec((1, BRANCH_W), lambda i: (0, 0))],
        out_specs=pl.BlockSpec((tm, BRANCH_W), lambda i: (i, 0)),
        out_shape=jax.ShapeDtypeStruct((t, BRANCH_W), BF16),
        compiler_params=_params("parallel"),
    )(zall, sg_w, sg_b.reshape(SG_GROUPS, SG_CHUNK, 1), ln_g.reshape(1, -1), ln_b.reshape(1, -1))


def _hgrn_kernel(z_ref, lb_ref, ng_ref, y_ref, st_ref, *, batch):
    c = HG_CHUNK

    @pl.when(pl.program_id(0) == 0)
    def _():
        st_ref[...] = jnp.zeros_like(st_ref)

    lb = lb_ref[...]
    ltri = jnp.where(_tril_mask(c), 1.0, 0.0).astype(BF16)
    rows = lax.broadcasted_iota(jnp.int32, (c, 1), 0)
    w = BRANCH_W
    for b in range(batch):
        zq = z_ref[b, :, 0:w]
        zf = z_ref[b, :, w:2 * w]
        zi = z_ref[b, :, 2 * w:3 * w]
        zg = z_ref[b, :, 3 * w:4 * w]
        log_f = _log_sigmoid(zf) + jnp.log(1.0 + lb * jnp.exp(jnp.minimum(-zf, HG_EXP_CLIP)))
        log_f = jnp.minimum(log_f, 0.0)
        kx = (1.0 - lb) * _sigmoid(-zf)
        q = zq * _sigmoid(zq)
        bc = _dot_exact_lhs(ltri, log_f)
        b_last = bc[c - 1:c, :]
        qd = q * jnp.exp(bc)
        kd = kx * jnp.exp(b_last - bc)
        e_last = jnp.exp(b_last)
        outs = []
        for h in range(HG_HEADS):
            sl = slice(h * HG_DK, (h + 1) * HG_DK)
            st = st_ref[b * HG_HEADS + h]
            o = _dot_nt(qd[:, sl], st)
            qh, kh, bh, vh = q[:, sl], kx[:, sl], bc[:, sl], zi[:, sl]
            parts = []
            for r0 in range(0, c, SUBLANES):
                rs = slice(r0, r0 + SUBLANES)
                qg, bg, og, rg = qh[rs], bh[rs], o[rs], rows[rs]
                for s in range(r0 + SUBLANES):
                    diff = bg - bh[s:s + 1, :]
                    if s > r0:
                        diff = jnp.minimum(diff, 0.0)
                    att = jnp.sum(qg * kh[s:s + 1, :] * jnp.exp(diff), axis=-1, keepdims=True)
                    if s > r0:
                        att = jnp.where(rg >= s, att, 0.0)
                    og = og + att * vh[s:s + 1, :]
                parts.append(og)
            o = jnp.concatenate(parts, axis=0)
            st_ref[b * HG_HEADS + h] = st * e_last[:, sl] + _dot_tn(vh, kd[:, sl])
            outs.append(o)
        o = jnp.concatenate(outs, axis=-1) * _sigmoid(zg)
        y = o * lax.rsqrt(jnp.mean(o * o, axis=-1, keepdims=True) + HG_EPS) * ng_ref[...]
        y_ref[b] = y.astype(BF16)


def hgrn2_branch(z3, col_block, lb, norm_g):
    batch, seq, _ = z3.shape
    return pl.pallas_call(
        functools.partial(_hgrn_kernel, batch=batch),
        grid=(seq // HG_CHUNK,),
        in_specs=[pl.BlockSpec((batch, HG_CHUNK, 4 * BRANCH_W), lambda c: (0, c, col_block)),
                  pl.BlockSpec((1, BRANCH_W), lambda c: (0, 0)),
                  pl.BlockSpec((1, BRANCH_W), lambda c: (0, 0))],
        out_specs=pl.BlockSpec((batch, HG_CHUNK, BRANCH_W), lambda c: (0, c, 0)),
        out_shape=jax.ShapeDtypeStruct((batch, seq, BRANCH_W), BF16),
        scratch_shapes=[pltpu.VMEM((batch * HG_HEADS, HG_DK, HG_DK), F32)],
        compiler_params=_params("arbitrary"),
    )(z3, lb.reshape(1, -1), norm_g.reshape(1, -1))


def _rw_prep_kernel(*refs, tiles_per_seq, has_vmix):
    if has_vmix:
        (z_ref, zp_ref, mu_ref, w0_ref, w2_ref, a0_ref, a2_ref, g2_ref, kk_ref, ka_ref, ones_ref,
         v0_ref, v1_ref, v2_ref, vf_ref, r_o, lw_o, k_o, v_o, kk_o, al_o, g_o) = refs
    else:
        (z_ref, zp_ref, mu_ref, w0_ref, w2_ref, a0_ref, a2_ref, g2_ref, kk_ref, ka_ref, ones_ref,
         r_o, lw_o, k_o, v_o, kk_o, al_o, g_o) = refs
    z = z_ref[...]
    tm = z.shape[0]
    first = (pl.program_id(0) % tiles_per_seq) == 0
    prev = jnp.where(first, 0.0, zp_ref[7:8, :])
    rows = lax.broadcasted_iota(jnp.int32, (tm, 1), 0)
    zs = jnp.where(rows == 0, prev, pltpu.roll(z, 1, 0))
    zc = z + (zs - z) * mu_ref[...]
    w = BRANCH_W
    p = RW_LORA_PAD
    r, k, v = zc[:, 0:w], zc[:, w:2 * w], zc[:, 2 * w:3 * w]
    wl, al_in, gl = zc[:, 3 * w:3 * w + p], zc[:, 3 * w + p:3 * w + 2 * p], zc[:, 3 * w + 2 * p:]
    u = w0_ref[...] + _dot(jnp.tanh(wl), w2_ref[...])
    lw = -math.exp(-0.5) * _sigmoid(u)
    a = _sigmoid(a0_ref[...] + _dot(al_in, a2_ref[...]))
    g = _dot(_sigmoid(gl), g2_ref[...])
    if has_vmix:
        mix = _sigmoid(v0_ref[...] + _dot(_dot(v, v1_ref[...]), v2_ref[...]))
        v = v + (vf_ref[...] - v) * mix
    kkx = k * kk_ref[...]
    norm = jnp.sqrt(_head_sum(kkx * kkx, ones_ref[...]))
    kkn = kkx / jnp.maximum(norm, 1e-12)
    r_o[...] = r
    lw_o[...] = lw
    k_o[...] = k * (1.0 + (a - 1.0) * ka_ref[...])
    v_o[...] = v
    kk_o[...] = kkn
    al_o[...] = a
    g_o[...] = g


def rw_prep(zall, col_block, seq, mu, w0, w2p, a0, a2p, g2, k_k, k_a, ones_blk, vmix, v_first, tm=256):
    t = zall.shape[0]
    tm = min(tm, seq)
    zw = 4 * BRANCH_W
    row = lambda i: (i, 0)
    const = lambda i: (0, 0)
    vec = pl.BlockSpec((1, BRANCH_W), const)
    in_specs = [pl.BlockSpec((tm, zw), lambda i: (i, col_block)),
                pl.BlockSpec((8, zw), lambda i: (jnp.maximum(i * (tm // 8) - 1, 0), col_block)),
                pl.BlockSpec((1, zw), const), vec,
                pl.BlockSpec((RW_LORA_PAD, BRANCH_W), const), vec,
                pl.BlockSpec((RW_LORA_PAD, BRANCH_W), const),
                pl.BlockSpec((RW_G_LORA, BRANCH_W), const), vec, vec,
                pl.BlockSpec((LANES, LANES), const)]
    args = [zall, zall, mu.reshape(1, -1), w0.reshape(1, -1), w2p, a0.reshape(1, -1), a2p, g2,
            k_k.reshape(1, -1), k_a.reshape(1, -1), ones_blk]
    if vmix is not None:
        v0, v1, v2 = vmix
        in_specs += [vec, pl.BlockSpec(v1.shape, const), pl.BlockSpec(v2.shape, const),
                     pl.BlockSpec((tm, BRANCH_W), row)]
        args += [v0.reshape(1, -1), v1, v2, v_first]
    out = jax.ShapeDtypeStruct((t, BRANCH_W), F32)
    return pl.pallas_call(
        functools.partial(_rw_prep_kernel, tiles_per_seq=seq // tm, has_vmix=vmix is not None),
        grid=(t // tm,),
        in_specs=in_specs,
        out_specs=[pl.BlockSpec((tm, BRANCH_W), row)] * 7,
        out_shape=[out] * 7,
        compiler_params=_params("parallel"),
    )(*args)


def _rwkv_kernel(r_ref, lw_ref, k_ref, v_ref, kk_ref, al_ref, g_ref, gng_ref, gnb_ref, rk_ref, ones_ref,
                 y_ref, st_ref, *, batch):
    c = RW_CHUNK

    @pl.when(pl.program_id(0) == 0)
    def _():
        st_ref[...] = jnp.zeros_like(st_ref)

    incl = _tril_mask(c)
    strict = _tril_mask(c, strict=True)
    ltri = jnp.where(incl, 1.0, 0.0).astype(BF16)
    units = [(b, h) for b in range(batch) for h in range(RW_HEADS)]
    ar, bk, vv, st, g_last = {}, {}, {}, {}, {}
    for b in range(batch):
        lw, kk = lw_ref[b], kk_ref[b]
        cum = _dot_exact_lhs(ltri, lw)
        gam = jnp.exp(cum)
        ginv = jnp.exp(-cum)
        at = -kk * jnp.exp(cum - lw)
        rt = r_ref[b] * gam
        bt = kk * al_ref[b] * ginv
        kt = k_ref[b] * ginv
        v = v_ref[b]
        for h in range(RW_HEADS):
            sl = slice(h * RW_HEAD, (h + 1) * RW_HEAD)
            ar[b, h] = jnp.concatenate([at[:, sl], rt[:, sl]], axis=0).astype(BF16)
            bk[b, h] = jnp.concatenate([bt[:, sl], kt[:, sl]], axis=0).astype(BF16)
            vv[b, h] = v[:, sl]
            st[b, h] = st_ref[b * RW_HEADS + h]
            g_last[b, h] = gam[c - 1:c, sl]
    nb = {u: _dot_nt(ar[u], bk[u][:c]) for u in units}
    nk = {u: _dot_nt(ar[u], bk[u][c:]) for u in units}
    a_s = {u: _dot_nt(ar[u], st[u]) for u in units}
    n = {u: jnp.where(strict, nb[u][:c], 0.0) for u in units}
    rb = {u: jnp.where(incl, nb[u][c:], 0.0) for u in units}
    akrk = {u: jnp.concatenate([jnp.where(strict, nk[u][:c], 0.0), jnp.where(incl, nk[u][c:], 0.0)], axis=0)
            for u in units}
    kv = {u: _dot(akrk[u], vv[u]) for u in units}
    x = {u: a_s[u][:c] + kv[u][:c] for u in units}
    right = lax.broadcasted_iota(jnp.int32, (c, 2 * c), 1) >= c
    w = {u: jnp.concatenate([n[u], x[u]], axis=1) for u in units}
    for _ in range(int(math.log2(c))):
        w = {u: jnp.where(right, w[u], 0.0) + _dot(w[u][:, :c], w[u]) for u in units}
    uu = {u: w[u][:, c:] for u in units}
    y = {u: a_s[u][c:] + kv[u][c:] + _dot(rb[u], uu[u]) for u in units}
    for u in units:
        upd = _dot_tn(jnp.concatenate([uu[u], vv[u]], axis=0), bk[u])
        st_ref[u[0] * RW_HEADS + u[1]] = (st[u] + upd) * g_last[u]
    ones = ones_ref[...]
    inv_n = 1.0 / RW_HEAD
    yb = [jnp.concatenate([y[b, h] for h in range(RW_HEADS)], axis=-1) for b in range(batch)]
    sums = _head_sums(yb + [r_ref[b] * k_ref[b] * rk_ref[...] for b in range(batch)], ones)
    yc = [yb[b] - sums[b] * inv_n for b in range(batch)]
    var = _head_sums([t * t for t in yc], ones)
    for b in range(batch):
        yn = yc[b] * lax.rsqrt(var[b] * inv_n + RW_GN_EPS) * gng_ref[...] + gnb_ref[...]
        y_ref[b] = ((yn + sums[batch + b] * v_ref[b]) * g_ref[b]).astype(BF16)


def rwkv7_mix(r, lw, k, v, kk, al, g, gn_g, gn_b, r_k, ones_blk, batch, seq):
    arrs = [x.reshape(batch, seq, BRANCH_W) for x in (r, lw, k, v, kk, al, g)]
    blk = pl.BlockSpec((batch, RW_CHUNK, BRANCH_W), lambda c: (0, c, 0))
    vec = pl.BlockSpec((1, BRANCH_W), lambda c: (0, 0))
    return pl.pallas_call(
        functools.partial(_rwkv_kernel, batch=batch),
        grid=(seq // RW_CHUNK,),
        in_specs=[blk] * 7 + [vec] * 3 + [pl.BlockSpec((LANES, LANES), lambda c: (0, 0))],
        out_specs=blk,
        out_shape=jax.ShapeDtypeStruct((batch, seq, BRANCH_W), BF16),
        scratch_shapes=[pltpu.VMEM((batch * RW_HEADS, RW_HEAD, RW_HEAD), F32)],
        compiler_params=_params("arbitrary"),
    )(*arrs, gn_g.reshape(1, -1), gn_b.reshape(1, -1), r_k.reshape(1, -1), ones_blk)


def _merge_kernel(x_ref, ysg_ref, yhg_ref, yrw_ref, wg_ref, wb_ref, o_ref):
    x = x_ref[...]
    acc = None
    for b, y_ref in enumerate((ysg_ref, yhg_ref, yrw_ref)):
        gate = _sigmoid(jnp.dot(x, wg_ref[b], preferred_element_type=F32))
        term = gate * jnp.dot(y_ref[...], wb_ref[b], preferred_element_type=F32)
        acc = term if acc is None else acc + term
    o_ref[...] = acc.astype(BF16)


def merge_branches(xb, y_sg, y_hg, y_rw, w_gate, w_branch, tm=1024, tn=512):
    t, d = xb.shape
    tm = min(tm, t)
    ysp = pl.BlockSpec((tm, BRANCH_W), lambda i, j: (i, 0))
    return pl.pallas_call(
        _merge_kernel,
        grid=(t // tm, d // tn),
        in_specs=[pl.BlockSpec((tm, d), lambda i, j: (i, 0)), ysp, ysp, ysp,
                  pl.BlockSpec((3, d, tn), lambda i, j: (0, 0, j)),
                  pl.BlockSpec((3, BRANCH_W, tn), lambda i, j: (0, 0, j))],
        out_specs=pl.BlockSpec((tm, tn), lambda i, j: (i, j)),
        out_shape=jax.ShapeDtypeStruct((t, d), BF16),
        compiler_params=_params("parallel", "parallel"),
    )(xb, y_sg, y_hg, y_rw, w_gate, w_branch)


def _xattn_kernel(xb_ref, xf_ref, k_ref, v_ref, wq_ref, wo_ref, g_ref, b_ref, of_ref, ob_ref, *, alpha):
    q = jnp.dot(xb_ref[0], wq_ref[...], preferred_element_type=F32)
    kmem, vmem = k_ref[0], v_ref[0]
    scale = XA_HEAD_DIM ** -0.5
    outs = []
    for h in range(XA_HEADS):
        sl = slice(h * XA_HEAD_DIM, (h + 1) * XA_HEAD_DIM)
        s = _dot_nt(q[:, sl], kmem[:, sl]) * scale
        e = jnp.exp(s - jnp.max(s, axis=-1, keepdims=True))
        p = e / jnp.sum(e, axis=-1, keepdims=True)
        outs.append(_dot(p, vmem[:, sl]))
    o = jnp.concatenate(outs, axis=-1)
    h_out = _dot(o, wo_ref[...])
    y = _layer_norm(alpha * xf_ref[0] + h_out, g_ref[...], b_ref[...], LN_EPS)
    of_ref[0] = y
    ob_ref[0] = y.astype(BF16)


def cross_attention_ln(xb, xf, kmem, vmem, wq, wo, g, b, alpha, tm=512):
    batch, seq, d = xf.shape
    m, xw = kmem.shape[1], kmem.shape[2]
    tm = min(tm, seq)
    xs = pl.BlockSpec((1, tm, d), lambda bi, i: (bi, i, 0))
    ms = pl.BlockSpec((1, m, xw), lambda bi, i: (bi, 0, 0))
    vec = pl.BlockSpec((1, d), lambda bi, i: (0, 0))
    return pl.pallas_call(
        functools.partial(_xattn_kernel, alpha=alpha),
        grid=(batch, seq // tm),
        in_specs=[xs, xs, ms, ms,
                  pl.BlockSpec((d, xw), lambda bi, i: (0, 0)),
                  pl.BlockSpec((xw, d), lambda bi, i: (0, 0)), vec, vec],
        out_specs=[xs, xs],
        out_shape=[jax.ShapeDtypeStruct((batch, seq, d), F32), jax.ShapeDtypeStruct((batch, seq, d), BF16)],
        compiler_params=_params("parallel", "parallel"),
    )(xb, xf, kmem, vmem, wq, wo, g.reshape(1, d), b.reshape(1, d))


def _ffn_kernel(be_ref, nu_ref, *refs, alpha, final_ln, gather, nf):
    del be_ref
    if gather:
        idx0_ref, idxn_ref, x_ref, w1_ref, w3_ref, w2_ref, g_ref, b_ref, of_ref, xb_scr, acc_scr, xg_scr, sems = refs
    elif final_ln:
        x_ref, w1_ref, w3_ref, w2_ref, g_ref, b_ref, of_ref, ob_ref, xb_scr, acc_scr = refs
    else:
        x_ref, w1_ref, w3_ref, w2_ref, g_ref, b_ref, of_ref, xb_scr, acc_scr = refs
    i = pl.program_id(0)
    f = pl.program_id(1)
    last = f == nf - 1
    n_used = nu_ref[0]
    active = i < n_used
    tm = xb_scr.shape[0]
    per_step = -(-tm // nf)
    slot = i % 2

    if gather:
        @pl.when((i == 0) & (f == 0))
        def _():
            _gather_rows(idx0_ref, x_ref, xg_scr.at[0], sems.at[0], tm)

        @pl.when((f == 0) & (i >= 1) & (i <= n_used))
        def _():
            def wait(r, carry):
                _row_copy(x_ref, xg_scr.at[slot], sems.at[slot], 0, r).wait()
                return carry
            lax.fori_loop(0, per_step * nf, wait, 0, unroll=DMA_UNROLL)

    @pl.when(active & (f == 0))
    def _():
        if gather:
            xb_scr[...] = xg_scr[slot, 0:tm, :].astype(BF16)
        else:
            xb_scr[...] = x_ref[...].astype(BF16)
        acc_scr[...] = jnp.zeros_like(acc_scr)

    @pl.when(active)
    def _():
        if gather:
            for j in range(per_step):
                r = f * per_step + j
                src_row = idxn_ref[0, 0, jnp.minimum(r, tm - 1)]
                _row_copy(x_ref, xg_scr.at[1 - slot], sems.at[1 - slot], src_row, r).start(priority=1)
        xb = xb_scr[...]
        gate = jnp.dot(xb, w1_ref[0], preferred_element_type=F32)
        up = jnp.dot(xb, w3_ref[0], preferred_element_type=F32)
        hh = (gate * _sigmoid(gate) * up).astype(BF16)
        acc_scr[...] += jnp.dot(hh, w2_ref[0], preferred_element_type=F32)

    @pl.when(active & last)
    def _():
        if final_ln:
            y = _layer_norm(alpha * x_ref[...] + acc_scr[...], g_ref[...], b_ref[...], LN_EPS)
            of_ref[...] = y
            ob_ref[...] = y.astype(BF16)
        else:
            of_ref[...] = acc_scr[...]

    @pl.when(jnp.logical_not(active) & last)
    def _():
        of_ref[...] = jnp.zeros_like(of_ref)
        if final_ln:
            ob_ref[...] = jnp.zeros_like(ob_ref)


def swiglu(x, block_expert, n_used, w1, w3, w2, g, b, alpha, final_ln, slot_rows=None, tm=512, tf=512):
    gather = slot_rows is not None
    assert not (gather and final_ln)
    d = x.shape[1]
    rows = slot_rows.shape[0] if gather else x.shape[0]
    ff = w1.shape[2]
    tm = min(tm, rows)
    nf = ff // tf
    n_blocks = rows // tm
    fi = lambda i, f, nu: jnp.where(i < nu[0], f, nf - 1)
    xs = pl.BlockSpec((tm, d), lambda i, f, be, nu: (i, 0))
    vec = pl.BlockSpec((1, d), lambda i, f, be, nu: (0, 0))
    weights = [pl.BlockSpec((1, d, tf), lambda i, f, be, nu: (be[i], 0, fi(i, f, nu))),
               pl.BlockSpec((1, d, tf), lambda i, f, be, nu: (be[i], 0, fi(i, f, nu))),
               pl.BlockSpec((1, tf, d), lambda i, f, be, nu: (be[i], fi(i, f, nu), 0)),
               vec, vec]
    scratch = [pltpu.VMEM((tm, d), BF16), pltpu.VMEM((tm, d), F32)]
    if gather:
        idx = slot_rows.reshape(n_blocks, 1, tm)
        smem_rows = lambda index_map: pl.BlockSpec((1, 1, tm), index_map, memory_space=pltpu.SMEM)
        in_specs = [smem_rows(lambda i, f, be, nu: (0, 0, 0)),
                    smem_rows(lambda i, f, be, nu: (jnp.minimum(i + 1, n_blocks - 1), 0, 0)),
                    pl.BlockSpec(memory_space=pl.ANY)] + weights
        args = (idx, idx, x)
        fetch_rows = -(-tm // nf) * nf
        scratch += [pltpu.VMEM((2, -(-fetch_rows // SUBLANES) * SUBLANES, d), F32), pltpu.SemaphoreType.DMA((2,))]
    else:
        in_specs = [pl.BlockSpec((tm, d), lambda i, f, be, nu: (jnp.minimum(i, nu[0] - 1), 0))] + weights
        args = (x,)
    n_out = 2 if final_ln else 1
    out_shape = [jax.ShapeDtypeStruct((rows, d), F32), jax.ShapeDtypeStruct((rows, d), BF16)][:n_out]
    res = pl.pallas_call(
        functools.partial(_ffn_kernel, alpha=alpha, final_ln=final_ln, gather=gather, nf=nf),
        grid_spec=pltpu.PrefetchScalarGridSpec(
            num_scalar_prefetch=2,
            grid=(n_blocks, nf),
            in_specs=in_specs,
            out_specs=[xs] * n_out,
            scratch_shapes=scratch),
        out_shape=out_shape,
        compiler_params=_params("arbitrary" if gather else "parallel", "arbitrary"),
    )(block_expert, n_used, *args, w1, w3, w2, g.reshape(1, d), b.reshape(1, d))
    return res if final_ln else res[0]


def _router_kernel(x_ref, w_ref, o_ref):
    x = x_ref[...]
    hi, mid, _ = _split3(x)
    whi, wmid, _ = _split3(w_ref[...])
    dot = lambda a, b: jnp.dot(a, b, preferred_element_type=F32)
    logits = dot(hi, whi) + (dot(hi, wmid) + dot(mid, whi))
    lane = lax.broadcasted_iota(jnp.int32, logits.shape, 1)
    neg = -jnp.inf
    lg = jnp.where(lane < N_EXPERTS, logits, neg)
    m1 = jnp.max(lg, axis=-1, keepdims=True)
    i1 = jnp.min(jnp.where(lg == m1, lane, LANES), axis=-1, keepdims=True)
    lg2 = jnp.where(lane == i1, neg, lg)
    m2 = jnp.max(lg2, axis=-1, keepdims=True)
    i2 = jnp.min(jnp.where(lg2 == m2, lane, LANES), axis=-1, keepdims=True)
    e = jnp.exp(m2 - m1)
    g1 = 1.0 / (1.0 + e)
    g2 = e / (1.0 + e)
    out = jnp.where(lane == 0, i1.astype(F32),
                    jnp.where(lane == 1, i2.astype(F32),
                              jnp.where(lane == 2, g1, jnp.where(lane == 3, g2, 0.0))))
    o_ref[...] = out


def route_top2(x, w_router_pad, tm=1024):
    t, d = x.shape
    tm = min(tm, t)
    return pl.pallas_call(
        _router_kernel,
        grid=(t // tm,),
        in_specs=[pl.BlockSpec((tm, d), lambda i: (i, 0)), pl.BlockSpec((d, LANES), lambda i: (0, 0))],
        out_specs=pl.BlockSpec((tm, LANES), lambda i: (i, 0)),
        out_shape=jax.ShapeDtypeStruct((t, LANES), F32),
        compiler_params=_params("parallel"),
    )(x, w_router_pad)


def _row_copy(src_hbm, dst_ref, sem, src_row, dst_row):
    return pltpu.make_async_copy(src_hbm.at[pl.ds(src_row, 1)], dst_ref.at[pl.ds(dst_row, 1)], sem)


def _gather_rows(idx_ref, src_hbm, dst_ref, sem, rows):
    def start(i, carry):
        for p in range(DMA_PRIORITIES):
            r = i * DMA_PRIORITIES + p
            _row_copy(src_hbm, dst_ref, sem, idx_ref[0, 0, r], r).start(priority=p)
        return carry

    def wait(r, carry):
        _row_copy(src_hbm, dst_ref, sem, 0, r).wait()
        return carry

    lax.fori_loop(0, rows // DMA_PRIORITIES, start, 0, unroll=DMA_UNROLL // DMA_PRIORITIES)
    lax.fori_loop(0, rows, wait, 0, unroll=DMA_UNROLL)


def _combine_kernel(idx0_ref, idxn_ref, y_hbm, x_ref, gate_ref, g_ref, b_ref, of_ref, ob_ref, ybuf, sems, *, alpha):
    tm = x_ref.shape[0]
    i = pl.program_id(0)
    slot = i % 2

    def wait_rows(s):
        def wait(r, carry):
            _row_copy(y_hbm, ybuf.at[s], sems.at[s], 0, r).wait()
            return carry
        lax.fori_loop(0, 2 * tm, wait, 0, unroll=DMA_UNROLL)

    @pl.when(i == 0)
    def _():
        _gather_rows(idx0_ref, y_hbm, ybuf.at[0], sems.at[0], 2 * tm)

    @pl.when(i >= 1)
    def _():
        wait_rows(slot)

    for r in range(2 * tm):
        _row_copy(y_hbm, ybuf.at[1 - slot], sems.at[1 - slot], idxn_ref[0, 0, r], r).start(
            priority=r % DMA_PRIORITIES)
    gates = gate_ref[...]
    h = gates[:, 0:1] * ybuf[slot, 0:tm, :] + gates[:, 1:2] * ybuf[slot, tm:2 * tm, :]
    y = _layer_norm(alpha * x_ref[...] + h, g_ref[...], b_ref[...], LN_EPS)
    of_ref[...] = y
    ob_ref[...] = y.astype(BF16)

    @pl.when(i == pl.num_programs(0) - 1)
    def _():
        wait_rows(1 - slot)


def moe_combine_ln(y_slots, dest, gates, x, g, b, alpha, tm=256):
    t, d = x.shape
    tm = min(tm, t)
    n_tiles = t // tm
    idx = dest.reshape(n_tiles, tm, 2).transpose(0, 2, 1).reshape(n_tiles, 1, 2 * tm)
    row = lambda i: (i, 0)
    vec = pl.BlockSpec((1, d), lambda i: (0, 0))
    smem_rows = lambda index_map: pl.BlockSpec((1, 1, 2 * tm), index_map, memory_space=pltpu.SMEM)
    return pl.pallas_call(
        functools.partial(_combine_kernel, alpha=alpha),
        grid=(n_tiles,),
        in_specs=[smem_rows(lambda i: (0, 0, 0)),
                  smem_rows(lambda i: (jnp.minimum(i + 1, n_tiles - 1), 0, 0)),
                  pl.BlockSpec(memory_space=pl.ANY),
                  pl.BlockSpec((tm, d), row), pl.BlockSpec((tm, 2), row), vec, vec],
        out_specs=[pl.BlockSpec((tm, d), row), pl.BlockSpec((tm, d), row)],
        out_shape=[jax.ShapeDtypeStruct((t, d), F32), jax.ShapeDtypeStruct((t, d), BF16)],
        scratch_shapes=[pltpu.VMEM((2, 2 * tm, d), F32), pltpu.SemaphoreType.DMA((2,))],
        compiler_params=_params("arbitrary"),
    )(idx, idx, y_slots, x, gates, g.reshape(1, d), b.reshape(1, d))


def moe_swiglu_ln(xf, w_router_pad, w1, w3, w2, g, b, alpha):
    t, d = xf.shape
    routed = route_top2(xf, w_router_pad)
    expert = routed[:, 0:2].astype(jnp.int32)
    gates = routed[:, 2:4]
    flat_e = expert.reshape(-1)
    onehot = (flat_e[:, None] == jnp.arange(N_EXPERTS, dtype=jnp.int32)[None, :]).astype(jnp.int32)
    rank = jnp.sum((jnp.cumsum(onehot, axis=0) - 1) * onehot, axis=1)
    counts = jnp.sum(onehot, axis=0)
    padded = (counts + MOE_ROWS - 1) // MOE_ROWS * MOE_ROWS
    ends = jnp.cumsum(padded)
    dest = (ends - padded)[flat_e] + rank
    n_blocks = (t * 2) // MOE_ROWS + N_EXPERTS
    n_slots = n_blocks * MOE_ROWS
    flat_tok = jnp.arange(t * 2, dtype=jnp.int32) // 2
    slot_tok = jnp.zeros((n_slots,), jnp.int32).at[dest].set(flat_tok)
    block_start = jnp.arange(n_blocks, dtype=jnp.int32) * MOE_ROWS
    block_expert = jnp.minimum(jnp.sum((ends[None, :] <= block_start[:, None]).astype(jnp.int32), axis=1),
                               N_EXPERTS - 1)
    n_used = (ends[N_EXPERTS - 1:] // MOE_ROWS).astype(jnp.int32)
    y_slots = swiglu(xf, block_expert, n_used, w1, w3, w2, g, b, alpha, final_ln=False, slot_rows=slot_tok,
                     tm=MOE_ROWS)
    return moe_combine_ln(y_slots, dest.reshape(t, 2).astype(jnp.int32), gates, xf, g, b, alpha)


def kernel(x, mem, w_in, sg_w, sg_b, sg_ln_g, sg_ln_b, hg_lb_logits, hg_norm_g, rw_mu, rw_w0, rw_w2, rw_a0, rw_a2, rw_g2, rw_k_k, rw_k_a, rw_r_k, rw_gn_g, rw_gn_b, rw_v0, rw_v1, rw_v2, w_branch, w_mix_out, xa_wq, xa_wk, xa_wv, xa_wo, ln_g, ln_b, ffn_w1, ffn_w3, ffn_w2, moe_router, moe_w1, moe_w3, moe_w2):
    batch, seq, d = x.shape
    depth = w_in.shape[0]
    t = batch * seq
    alpha = (2 * depth) ** 0.25
    w = BRANCH_W
    sg_cols, hg_cols = 2 * w, 4 * w
    rw_cols = 3 * w + RW_W_LORA + RW_A_LORA + RW_G_LORA
    o_hg, o_rw, o_gate = sg_cols, sg_cols + hg_cols, sg_cols + hg_cols + rw_cols

    p = jax.nn.softmax(hg_lb_logits.astype(F32), axis=0)
    lower_bounds = jnp.cumsum(p, axis=0) - p[0]
    lane_head = jnp.arange(LANES) // RW_HEAD
    ones_blk = (lane_head[:, None] == lane_head[None, :]).astype(BF16)

    def pad_cols(m, n):
        return jnp.pad(m, ((0, 0), (0, n - m.shape[1])))

    def pad_rows(m, n):
        return jnp.pad(m, ((0, n - m.shape[0]), (0, 0)))

    mem_b = mem.reshape(-1, d).astype(BF16)
    xf = x.reshape(t, d)
    xb = xf.astype(BF16)
    v_first = None
    for l in range(depth):
        wl_ = w_in[l]
        c0 = o_rw + 3 * w
        w_rw = jnp.concatenate([wl_[:, o_rw:c0],
                                pad_cols(wl_[:, c0:c0 + RW_W_LORA], RW_LORA_PAD),
                                pad_cols(wl_[:, c0 + RW_W_LORA:c0 + RW_W_LORA + RW_A_LORA], RW_LORA_PAD),
                                wl_[:, c0 + RW_W_LORA + RW_A_LORA:o_gate]], axis=1)
        w_proj = jnp.concatenate([wl_[:, o_hg:o_rw], w_rw, wl_[:, :o_hg]], axis=1).astype(BF16)
        w_gate = wl_[:, o_gate:].reshape(d, 3, d).transpose(1, 0, 2).astype(BF16)
        mu = rw_mu[l]
        mu_p = jnp.concatenate([mu[:3 * w],
                                jnp.pad(mu[3 * w:3 * w + RW_W_LORA], (0, RW_LORA_PAD - RW_W_LORA)),
                                jnp.pad(mu[3 * w + RW_W_LORA:3 * w + RW_W_LORA + RW_A_LORA],
                                        (0, RW_LORA_PAD - RW_A_LORA)),
                                mu[3 * w + RW_W_LORA + RW_A_LORA:]])

        zall = matmul(xb, w_proj, F32, tm=2048, tn=1024)
        y_sg = sg_branch(zall, 4, sg_w[l], sg_b[l], sg_ln_g[l], sg_ln_b[l])
        y_hg = hgrn2_branch(zall.reshape(batch, seq, -1), 0, lower_bounds[l], hg_norm_g[l]).reshape(t, w)
        vmix = None if l == 0 else (rw_v0[l - 1], rw_v1[l - 1].astype(BF16), rw_v2[l - 1].astype(BF16))
        r_, lw_, k_, v_, kk_, al_, g_ = rw_prep(
            zall, 1, seq, mu_p, rw_w0[l], pad_rows(rw_w2[l], RW_LORA_PAD).astype(BF16), rw_a0[l],
            pad_rows(rw_a2[l], RW_LORA_PAD).astype(BF16), rw_g2[l].astype(BF16), rw_k_k[l], rw_k_a[l],
            ones_blk, vmix, v_first)
        if l == 0:
            v_first = v_
        y_rw = rwkv7_mix(r_, lw_, k_, v_, kk_, al_, g_, rw_gn_g[l], rw_gn_b[l], rw_r_k[l], ones_blk,
                         batch, seq).reshape(t, w)
        merged = merge_branches(xb, y_sg, y_hg, y_rw, w_gate, w_branch[l].astype(BF16))
        xf, xb = matmul_res_ln(merged, w_mix_out[l].astype(BF16), xf, ln_g[l, 0], ln_b[l, 0], alpha)

        kmem = matmul(mem_b, xa_wk[l].astype(BF16), BF16).reshape(batch, -1, XA_HEADS * XA_HEAD_DIM)
        vmem = matmul(mem_b, xa_wv[l].astype(BF16), BF16).reshape(batch, -1, XA_HEADS * XA_HEAD_DIM)
        xf3, xb3 = cross_attention_ln(xb.reshape(batch, seq, d), xf.reshape(batch, seq, d), kmem, vmem,
                                      xa_wq[l].astype(BF16), xa_wo[l].astype(BF16), ln_g[l, 1], ln_b[l, 1], alpha)
        xf, xb = xf3.reshape(t, d), xb3.reshape(t, d)

        if l % 2 == 0:
            i = l // 2
            n_row_blocks = t // min(512, t)
            xf, xb = swiglu(xf, jnp.zeros((n_row_blocks,), jnp.int32), jnp.full((1,), n_row_blocks, jnp.int32),
                            ffn_w1[i:i + 1].astype(BF16),
                            ffn_w3[i:i + 1].astype(BF16), ffn_w2[i:i + 1].astype(BF16),
                            ln_g[l, 2], ln_b[l, 2], alpha, final_ln=True)
        else:
            i = l // 2
            xf, xb = moe_swiglu_ln(xf, pad_cols(moe_router[i], LANES), moe_w1[i].astype(BF16),
                                   moe_w3[i].astype(BF16), moe_w2[i].astype(BF16), ln_g[l, 2], ln_b[l, 2], alpha)
    return xf.reshape(batch, seq, d)
```

```python
import functools
import math

import jax
import jax.numpy as jnp
from jax import lax
from jax.experimental import pallas as pl
from jax.experimental.pallas import tpu as pltpu

F32 = jnp.float32
BF16 = jnp.bfloat16

V7X_VMEM_BYTES = 64 * 1024 * 1024
VMEM_LIMIT = V7X_VMEM_BYTES - 8 * 1024 * 1024
LANES = 128
SUBLANES = 8

BRANCH_W = 512
SG_CHUNK = 128
SG_GROUPS = 4
HG_HEADS = 4
HG_DK = BRANCH_W // HG_HEADS
HG_CHUNK = 16
HG_EPS = 1e-6
HG_EXP_CLIP = 60.0
RW_HEAD = 64
RW_HEADS = BRANCH_W // RW_HEAD
RW_CHUNK = 64
RW_W_LORA = 96
RW_A_LORA = 96
RW_G_LORA = 256
RW_GN_EPS = 64e-5
RW_LORA_PAD = 128
XA_HEADS = 4
XA_HEAD_DIM = 128
N_EXPERTS = 8
MOE_ROWS = 512
LN_EPS = 1e-5
DMA_UNROLL = 8
DMA_PRIORITIES = 2


def _params(*sem):
    return pltpu.CompilerParams(dimension_semantics=sem, vmem_limit_bytes=VMEM_LIMIT)


def _dot(a, b):
    return jnp.dot(a.astype(BF16), b.astype(BF16), preferred_element_type=F32)


def _dot_nt(a, b):
    return lax.dot_general(a.astype(BF16), b.astype(BF16), (((1,), (1,)), ((), ())),
                           preferred_element_type=F32)


def _dot_tn(a, b):
    return lax.dot_general(a.astype(BF16), b.astype(BF16), (((0,), (0,)), ((), ())),
                           preferred_element_type=F32)


def _split3(x):
    hi = x.astype(BF16)
    r1 = x - hi.astype(F32)
    mid = r1.astype(BF16)
    lo = (r1 - mid.astype(F32)).astype(BF16)
    return hi, mid, lo


def _dot_exact_lhs(m_bf16, x):
    hi, mid, lo = _split3(x)
    return (jnp.dot(m_bf16, hi, preferred_element_type=F32)
            + jnp.dot(m_bf16, mid, preferred_element_type=F32)
            + jnp.dot(m_bf16, lo, preferred_element_type=F32))


def _head_sums(xs, ones_blk):
    rows = xs[0].shape[0]
    n = xs[0].shape[-1] // LANES
    parts = [p for x in xs for i in range(n) for p in _split3(x[:, i * LANES:(i + 1) * LANES])]
    s = jnp.dot(jnp.concatenate(parts, axis=0), ones_blk, preferred_element_type=F32)
    piece = lambda j: s[j * rows:(j + 1) * rows]
    return [jnp.concatenate([piece(3 * (a * n + i)) + piece(3 * (a * n + i) + 1) + piece(3 * (a * n + i) + 2)
                             for i in range(n)], axis=-1) for a in range(len(xs))]


def _head_sum(x, ones_blk):
    return _head_sums([x], ones_blk)[0]


def _sigmoid(x):
    return 1.0 / (1.0 + jnp.exp(-x))


def _log_sigmoid(x):
    return jnp.minimum(x, 0.0) - jnp.log(1.0 + jnp.exp(-jnp.abs(x)))


def _layer_norm(x, g, b, eps):
    mu = jnp.mean(x, axis=-1, keepdims=True)
    xc = x - mu
    var = jnp.mean(xc * xc, axis=-1, keepdims=True)
    return xc * lax.rsqrt(var + eps) * g + b


def _tril_mask(n, strict=False):
    row = lax.broadcasted_iota(jnp.int32, (n, n), 0)
    col = lax.broadcasted_iota(jnp.int32, (n, n), 1)
    return (row > col) if strict else (row >= col)


def _mm_kernel(a_ref, b_ref, o_ref):
    o_ref[...] = jnp.dot(a_ref[...], b_ref[...], preferred_element_type=F32).astype(o_ref.dtype)


def matmul(a, b, out_dtype, tm=1024, tn=512):
    m, k = a.shape
    n = b.shape[1]
    tm, tn = min(tm, m), min(tn, n)
    return pl.pallas_call(
        _mm_kernel,
        grid=(m // tm, n // tn),
        in_specs=[pl.BlockSpec((tm, k), lambda i, j: (i, 0)),
                  pl.BlockSpec((k, tn), lambda i, j: (0, j))],
        out_specs=pl.BlockSpec((tm, tn), lambda i, j: (i, j)),
        out_shape=jax.ShapeDtypeStruct((m, n), out_dtype),
        compiler_params=_params("parallel", "parallel"),
    )(a, b)


def _mm_res_ln_kernel(a_ref, w_ref, res_ref, g_ref, b_ref, of_ref, ob_ref, *, alpha):
    half = a_ref.shape[0] // 2
    for r in range(2):
        rs = slice(r * half, (r + 1) * half)
        h = jnp.dot(a_ref[rs, :], w_ref[...], preferred_element_type=F32)
        y = _layer_norm(alpha * res_ref[rs, :] + h, g_ref[...], b_ref[...], LN_EPS)
        of_ref[rs, :] = y
        ob_ref[rs, :] = y.astype(BF16)


def matmul_res_ln(a, w, res, g, b, alpha, tm=512):
    m, k = a.shape
    n = w.shape[1]
    tm = min(tm, m)
    return pl.pallas_call(
        functools.partial(_mm_res_ln_kernel, alpha=alpha),
        grid=(m // tm,),
        in_specs=[pl.BlockSpec((tm, k), lambda i: (i, 0)),
                  pl.BlockSpec((k, n), lambda i: (0, 0)),
                  pl.BlockSpec((tm, n), lambda i: (i, 0)),
                  pl.BlockSpec((1, n), lambda i: (0, 0)),
                  pl.BlockSpec((1, n), lambda i: (0, 0))],
        out_specs=[pl.BlockSpec((tm, n), lambda i: (i, 0)),
                   pl.BlockSpec((tm, n), lambda i: (i, 0))],
        out_shape=[jax.ShapeDtypeStruct((m, n), F32), jax.ShapeDtypeStruct((m, n), BF16)],
        compiler_params=_params("parallel"),
    )(a, w, res, g.reshape(1, n), b.reshape(1, n))


def _sg_kernel(z_ref, w_ref, b_ref, g_ref, beta_ref, y_ref, *, n_chunks):
    z = z_ref[...]
    z = 0.5 * z * (1.0 + jnp.tanh(math.sqrt(2.0 / math.pi) * (z + 0.044715 * (z * z * z))))
    u = z[:, :BRANCH_W]
    v = _layer_norm(z[:, BRANCH_W:], g_ref[...], beta_ref[...], LN_EPS).astype(BF16)
    causal = _tril_mask(SG_CHUNK)
    gw = BRANCH_W // SG_GROUPS
    for g in range(SG_GROUPS):
        wg = jnp.where(causal, w_ref[g], 0.0).astype(BF16)
        for c in range(n_chunks):
            rows = slice(c * SG_CHUNK, (c + 1) * SG_CHUNK)
            cols = slice(g * gw, (g + 1) * gw)
            s = jnp.dot(wg, v[rows, cols], preferred_element_type=F32) + b_ref[g]
            y_ref[rows, cols] = (u[rows, cols] * s).astype(BF16)


def sg_branch(zall, col_block, sg_w, sg_b, ln_g, ln_b, tm=512):
    t = zall.shape[0]
    tm = min(tm, t)
    return pl.pallas_call(
        functools.partial(_sg_kernel, n_chunks=tm // SG_CHUNK),
        grid=(t // tm,),
        in_specs=[pl.BlockSpec((tm, 2 * BRANCH_W), lambda i: (i, col_block)),
                  pl.BlockSpec((SG_GROUPS, SG_CHUNK, SG_CHUNK), lambda i: (0, 0, 0)),
                  pl.BlockSpec((SG_GROUPS, SG_CHUNK, 1), lambda i: (0, 0, 0)),
                  pl.BlockSpec((1, BRANCH_W), lambda i: (0, 0)),
                  pl.BlockSpec((1, BRANCH_W), lambda i: (0, 0))],
        out_specs=pl.BlockSpec((tm, BRANCH_W), lambda i: (i, 0)),
        out_shape=jax.ShapeDtypeStruct((t, BRANCH_W), BF16),
        compiler_params=_params("parallel"),
    )(zall, sg_w, sg_b.reshape(SG_GROUPS, SG_CHUNK, 1), ln_g.reshape(1, -1), ln_b.reshape(1, -1))


def _hgrn_kernel(z_ref, lb_ref, ng_ref, y_ref, st_ref, *, batch):
    c = HG_CHUNK

    @pl.when(pl.program_id(0) == 0)
    def _():
        st_ref[...] = jnp.zeros_like(st_ref)

    lb = lb_ref[...]
    ltri = jnp.where(_tril_mask(c), 1.0, 0.0).astype(BF16)
    rows = lax.broadcasted_iota(jnp.int32, (c, 1), 0)
    w = BRANCH_W
    for b in range(batch):
        zq = z_ref[b, :, 0:w]
        zf = z_ref[b, :, w:2 * w]
        zi = z_ref[b, :, 2 * w:3 * w]
        zg = z_ref[b, :, 3 * w:4 * w]
        log_f = _log_sigmoid(zf) + jnp.log(1.0 + lb * jnp.exp(jnp.minimum(-zf, HG_EXP_CLIP)))
        log_f = jnp.minimum(log_f, 0.0)
        kx = (1.0 - lb) * _sigmoid(-zf)
        q = zq * _sigmoid(zq)
        bc = _dot_exact_lhs(ltri, log_f)
        b_last = bc[c - 1:c, :]
        qd = q * jnp.exp(bc)
        kd = kx * jnp.exp(b_last - bc)
        e_last = jnp.exp(b_last)
        outs = []
        for h in range(HG_HEADS):
            sl = slice(h * HG_DK, (h + 1) * HG_DK)
            st = st_ref[b * HG_HEADS + h]
            o = _dot_nt(qd[:, sl], st)
            qh, kh, bh, vh = q[:, sl], kx[:, sl], bc[:, sl], zi[:, sl]
            parts = []
            for r0 in range(0, c, SUBLANES):
                rs = slice(r0, r0 + SUBLANES)
                qg, bg, og, rg = qh[rs], bh[rs], o[rs], rows[rs]
                for s in range(r0 + SUBLANES):
                    diff = bg - bh[s:s + 1, :]
                    if s > r0:
                        diff = jnp.minimum(diff, 0.0)
                    att = jnp.sum(qg * kh[s:s + 1, :] * jnp.exp(diff), axis=-1, keepdims=True)
                    if s > r0:
                        att = jnp.where(rg >= s, att, 0.0)
                    og = og + att * vh[s:s + 1, :]
                parts.append(og)
            o = jnp.concatenate(parts, axis=0)
            st_ref[b * HG_HEADS + h] = st * e_last[:, sl] + _dot_tn(vh, kd[:, sl])
            outs.append(o)
        o = jnp.concatenate(outs, axis=-1) * _sigmoid(zg)
        y = o * lax.rsqrt(jnp.mean(o * o, axis=-1, keepdims=True) + HG_EPS) * ng_ref[...]
        y_ref[b] = y.astype(BF16)


def hgrn2_branch(z3, col_block, lb, norm_g):
    batch, seq, _ = z3.shape
    return pl.pallas_call(
        functools.partial(_hgrn_kernel, batch=batch),
        grid=(seq // HG_CHUNK,),
        in_specs=[pl.BlockSpec((batch, HG_CHUNK, 4 * BRANCH_W), lambda c: (0, c, col_block)),
                  pl.BlockSpec((1, BRANCH_W), lambda c: (0, 0)),
                  pl.BlockSpec((1, BRANCH_W), lambda c: (0, 0))],
        out_specs=pl.BlockSpec((batch, HG_CHUNK, BRANCH_W), lambda c: (0, c, 0)),
        out_shape=jax.ShapeDtypeStruct((batch, seq, BRANCH_W), BF16),
        scratch_shapes=[pltpu.VMEM((batch * HG_HEADS, HG_DK, HG_DK), F32)],
        compiler_params=_params("arbitrary"),
    )(z3, lb.reshape(1, -1), norm_g.reshape(1, -1))


def _rw_prep_kernel(*refs, tiles_per_seq, has_vmix):
    if has_vmix:
        (z_ref, zp_ref, mu_ref, w0_ref, w2_ref, a0_ref, a2_ref, g2_ref, kk_ref, ka_ref, ones_ref,
         v0_ref, v1_ref, v2_ref, vf_ref, r_o, lw_o, k_o, v_o, kk_o, al_o, g_o) = refs
    else:
        (z_ref, zp_ref, mu_ref, w0_ref, w2_ref, a0_ref, a2_ref, g2_ref, kk_ref, ka_ref, ones_ref,
         r_o, lw_o, k_o, v_o, kk_o, al_o, g_o) = refs
    z = z_ref[...]
    tm = z.shape[0]
    first = (pl.program_id(0) % tiles_per_seq) == 0
    prev = jnp.where(first, 0.0, zp_ref[7:8, :])
    rows = lax.broadcasted_iota(jnp.int32, (tm, 1), 0)
    zs = jnp.where(rows == 0, prev, pltpu.roll(z, 1, 0))
    zc = z + (zs - z) * mu_ref[...]
    w = BRANCH_W
    p = RW_LORA_PAD
    r, k, v = zc[:, 0:w], zc[:, w:2 * w], zc[:, 2 * w:3 * w]
    wl, al_in, gl = zc[:, 3 * w:3 * w + p], zc[:, 3 * w + p:3 * w + 2 * p], zc[:, 3 * w + 2 * p:]
    u = w0_ref[...] + _dot(jnp.tanh(wl), w2_ref[...])
    lw = -math.exp(-0.5) * _sigmoid(u)
    a = _sigmoid(a0_ref[...] + _dot(al_in, a2_ref[...]))
    g = _dot(_sigmoid(gl), g2_ref[...])
    if has_vmix:
        mix = _sigmoid(v0_ref[...] + _dot(_dot(v, v1_ref[...]), v2_ref[...]))
        v = v + (vf_ref[...] - v) * mix
    kkx = k * kk_ref[...]
    norm = jnp.sqrt(_head_sum(kkx * kkx, ones_ref[...]))
    kkn = kkx / jnp.maximum(norm, 1e-12)
    r_o[...] = r
    lw_o[...] = lw
    k_o[...] = k * (1.0 + (a - 1.0) * ka_ref[...])
    v_o[...] = v
    kk_o[...] = kkn
    al_o[...] = a
    g_o[...] = g


def rw_prep(zall, col_block, seq, mu, w0, w2p, a0, a2p, g2, k_k, k_a, ones_blk, vmix, v_first, tm=256):
    t = zall.shape[0]
    tm = min(tm, seq)
    zw = 4 * BRANCH_W
    row = lambda i: (i, 0)
    const = lambda i: (0, 0)
    vec = pl.BlockSpec((1, BRANCH_W), const)
    in_specs = [pl.BlockSpec((tm, zw), lambda i: (i, col_block)),
                pl.BlockSpec((8, zw), lambda i: (jnp.maximum(i * (tm // 8) - 1, 0), col_block)),
                pl.BlockSpec((1, zw), const), vec,
                pl.BlockSpec((RW_LORA_PAD, BRANCH_W), const), vec,
                pl.BlockSpec((RW_LORA_PAD, BRANCH_W), const),
                pl.BlockSpec((RW_G_LORA, BRANCH_W), const), vec, vec,
                pl.BlockSpec((LANES, LANES), const)]
    args = [zall, zall, mu.reshape(1, -1), w0.reshape(1, -1), w2p, a0.reshape(1, -1), a2p, g2,
            k_k.reshape(1, -1), k_a.reshape(1, -1), ones_blk]
    if vmix is not None:
        v0, v1, v2 = vmix
        in_specs += [vec, pl.BlockSpec(v1.shape, const), pl.BlockSpec(v2.shape, const),
                     pl.BlockSpec((tm, BRANCH_W), row)]
        args += [v0.reshape(1, -1), v1, v2, v_first]
    out = jax.ShapeDtypeStruct((t, BRANCH_W), F32)
    return pl.pallas_call(
        functools.partial(_rw_prep_kernel, tiles_per_seq=seq // tm, has_vmix=vmix is not None),
        grid=(t // tm,),
        in_specs=in_specs,
        out_specs=[pl.BlockSpec((tm, BRANCH_W), row)] * 7,
        out_shape=[out] * 7,
        compiler_params=_params("parallel"),
    )(*args)


def _rwkv_kernel(r_ref, lw_ref, k_ref, v_ref, kk_ref, al_ref, g_ref, gng_ref, gnb_ref, rk_ref, ones_ref,
                 y_ref, st_ref, *, batch):
    c = RW_CHUNK

    @pl.when(pl.program_id(0) == 0)
    def _():
        st_ref[...] = jnp.zeros_like(st_ref)

    incl = _tril_mask(c)
    strict = _tril_mask(c, strict=True)
    ltri = jnp.where(incl, 1.0, 0.0).astype(BF16)
    units = [(b, h) for b in range(batch) for h in range(RW_HEADS)]
    ar, bk, vv, st, g_last = {}, {}, {}, {}, {}
    for b in range(batch):
        lw, kk = lw_ref[b], kk_ref[b]
        cum = _dot_exact_lhs(ltri, lw)
        gam = jnp.exp(cum)
        ginv = jnp.exp(-cum)
        at = -kk * jnp.exp(cum - lw)
        rt = r_ref[b] * gam
        bt = kk * al_ref[b] * ginv
        kt = k_ref[b] * ginv
        v = v_ref[b]
        for h in range(RW_HEADS):
            sl = slice(h * RW_HEAD, (h + 1) * RW_HEAD)
            ar[b, h] = jnp.concatenate([at[:, sl], rt[:, sl]], axis=0).astype(BF16)
            bk[b, h] = jnp.concatenate([bt[:, sl], kt[:, sl]], axis=0).astype(BF16)
            vv[b, h] = v[:, sl]
            st[b, h] = st_ref[b * RW_HEADS + h]
            g_last[b, h] = gam[c - 1:c, sl]
    nb = {u: _dot_nt(ar[u], bk[u][:c]) for u in units}
    nk = {u: _dot_nt(ar[u], bk[u][c:]) for u in units}
    a_s = {u: _dot_nt(ar[u], st[u]) for u in units}
    n = {u: jnp.where(strict, nb[u][:c], 0.0) for u in units}
    rb = {u: jnp.where(incl, nb[u][c:], 0.0) for u in units}
    akrk = {u: jnp.concatenate([jnp.where(strict, nk[u][:c], 0.0), jnp.where(incl, nk[u][c:], 0.0)], axis=0)
            for u in units}
    kv = {u: _dot(akrk[u], vv[u]) for u in units}
    x = {u: a_s[u][:c] + kv[u][:c] for u in units}
    right = lax.broadcasted_iota(jnp.int32, (c, 2 * c), 1) >= c
    w = {u: jnp.concatenate([n[u], x[u]], axis=1) for u in units}
    for _ in range(int(math.log2(c))):
        w = {u: jnp.where(right, w[u], 0.0) + _dot(w[u][:, :c], w[u]) for u in units}
    uu = {u: w[u][:, c:] for u in units}
    y = {u: a_s[u][c:] + kv[u][c:] + _dot(rb[u], uu[u]) for u in units}
    for u in units:
        upd = _dot_tn(jnp.concatenate([uu[u], vv[u]], axis=0), bk[u])
        st_ref[u[0] * RW_HEADS + u[1]] = (st[u] + upd) * g_last[u]
    ones = ones_ref[...]
    inv_n = 1.0 / RW_HEAD
    yb = [jnp.concatenate([y[b, h] for h in range(RW_HEADS)], axis=-1) for b in range(batch)]
    sums = _head_sums(yb + [r_ref[b] * k_ref[b] * rk_ref[...] for b in range(batch)], ones)
    yc = [yb[b] - sums[b] * inv_n for b in range(batch)]
    var = _head_sums([t * t for t in yc], ones)
    for b in range(batch):
        yn = yc[b] * lax.rsqrt(var[b] * inv_n + RW_GN_EPS) * gng_ref[...] + gnb_ref[...]
        y_ref[b] = ((yn + sums[batch + b] * v_ref[b]) * g_ref[b]).astype(BF16)


def rwkv7_mix(r, lw, k, v, kk, al, g, gn_g, gn_b, r_k, ones_blk, batch, seq):
    arrs = [x.reshape(batch, seq, BRANCH_W) for x in (r, lw, k, v, kk, al, g)]
    blk = pl.BlockSpec((batch, RW_CHUNK, BRANCH_W), lambda c: (0, c, 0))
    vec = pl.BlockSpec((1, BRANCH_W), lambda c: (0, 0))
    return pl.pallas_call(
        functools.partial(_rwkv_kernel, batch=batch),
        grid=(seq // RW_CHUNK,),
        in_specs=[blk] * 7 + [vec] * 3 + [pl.BlockSpec((LANES, LANES), lambda c: (0, 0))],
        out_specs=blk,
        out_shape=jax.ShapeDtypeStruct((batch, seq, BRANCH_W), BF16),
        scratch_shapes=[pltpu.VMEM((batch * RW_HEADS, RW_HEAD, RW_HEAD), F32)],
        compiler_params=_params("arbitrary"),
    )(*arrs, gn_g.reshape(1, -1), gn_b.reshape(1, -1), r_k.reshape(1, -1), ones_blk)


def _merge_kernel(x_ref, ysg_ref, yhg_ref, yrw_ref, wg_ref, wb_ref, o_ref):
    x = x_ref[...]
    acc = None
    for b, y_ref in enumerate((ysg_ref, yhg_ref, yrw_ref)):
        gate = _sigmoid(jnp.dot(x, wg_ref[b], preferred_element_type=F32))
        term = gate * jnp.dot(y_ref[...], wb_ref[b], preferred_element_type=F32)
        acc = term if acc is None else acc + term
    o_ref[...] = acc.astype(BF16)


def merge_branches(xb, y_sg, y_hg, y_rw, w_gate, w_branch, tm=1024, tn=512):
    t, d = xb.shape
    tm = min(tm, t)
    ysp = pl.BlockSpec((tm, BRANCH_W), lambda i, j: (i, 0))
    return pl.pallas_call(
        _merge_kernel,
        grid=(t // tm, d // tn),
        in_specs=[pl.BlockSpec((tm, d), lambda i, j: (i, 0)), ysp, ysp, ysp,
                  pl.BlockSpec((3, d, tn), lambda i, j: (0, 0, j)),
                  pl.BlockSpec((3, BRANCH_W, tn), lambda i, j: (0, 0, j))],
        out_specs=pl.BlockSpec((tm, tn), lambda i, j: (i, j)),
        out_shape=jax.ShapeDtypeStruct((t, d), BF16),
        compiler_params=_params("parallel", "parallel"),
    )(xb, y_sg, y_hg, y_rw, w_gate, w_branch)


def _xattn_kernel(xb_ref, xf_ref, k_ref, v_ref, wq_ref, wo_ref, g_ref, b_ref, of_ref, ob_ref, *, alpha):
    q = jnp.dot(xb_ref[0], wq_ref[...], preferred_element_type=F32)
    kmem, vmem = k_ref[0], v_ref[0]
    scale = XA_HEAD_DIM ** -0.5
    heads = [slice(h * XA_HEAD_DIM, (h + 1) * XA_HEAD_DIM) for h in range(XA_HEADS)]
    scores = [_dot_nt(q[:, sl], kmem[:, sl]) * scale for sl in heads]
    probs = []
    for s in scores:
        e = jnp.exp(s - jnp.max(s, axis=-1, keepdims=True))
        probs.append(e / jnp.sum(e, axis=-1, keepdims=True))
    o = jnp.concatenate([_dot(p, vmem[:, sl]) for p, sl in zip(probs, heads)], axis=-1)
    o = o.astype(BF16)
    half = o.shape[0] // 2
    for r in range(2):
        rs = slice(r * half, (r + 1) * half)
        h_out = jnp.dot(o[rs], wo_ref[...], preferred_element_type=F32)
        y = _layer_norm(alpha * xf_ref[0, rs, :] + h_out, g_ref[...], b_ref[...], LN_EPS)
        of_ref[0, rs, :] = y
        ob_ref[0, rs, :] = y.astype(BF16)


def cross_attention_ln(xb, xf, kmem, vmem, wq, wo, g, b, alpha, tm=512):
    batch, seq, d = xf.shape
    m, xw = kmem.shape[1], kmem.shape[2]
    tm = min(tm, seq)
    xs = pl.BlockSpec((1, tm, d), lambda bi, i: (bi, i, 0))
    ms = pl.BlockSpec((1, m, xw), lambda bi, i: (bi, 0, 0))
    vec = pl.BlockSpec((1, d), lambda bi, i: (0, 0))
    return pl.pallas_call(
        functools.partial(_xattn_kernel, alpha=alpha),
        grid=(batch, seq // tm),
        in_specs=[xs, xs, ms, ms,
                  pl.BlockSpec((d, xw), lambda bi, i: (0, 0)),
                  pl.BlockSpec((xw, d), lambda bi, i: (0, 0)), vec, vec],
        out_specs=[xs, xs],
        out_shape=[jax.ShapeDtypeStruct((batch, seq, d), F32), jax.ShapeDtypeStruct((batch, seq, d), BF16)],
        compiler_params=_params("parallel", "parallel"),
    )(xb, xf, kmem, vmem, wq, wo, g.reshape(1, d), b.reshape(1, d))


def _ffn_kernel(be_ref, nu_ref, *refs, alpha, final_ln, gather, nf):
    del be_ref
    if gather:
        idx0_ref, idxn_ref, x_ref, w1_ref, w3_ref, w2_ref, g_ref, b_ref, of_ref, xb_scr, acc_scr, xg_scr, sems = refs
    elif final_ln:
        x_ref, w1_ref, w3_ref, w2_ref, g_ref, b_ref, of_ref, ob_ref, xb_scr, acc_scr = refs
    else:
        x_ref, w1_ref, w3_ref, w2_ref, g_ref, b_ref, of_ref, xb_scr, acc_scr = refs
    i = pl.program_id(0)
    f = pl.program_id(1)
    last = f == nf - 1
    n_used = nu_ref[0]
    active = i < n_used
    tm = xb_scr.shape[0]
    per_step = -(-tm // nf)
    slot = i % 2

    if gather:
        @pl.when((i == 0) & (f == 0))
        def _():
            _gather_rows(idx0_ref, x_ref, xg_scr.at[0], sems.at[0], tm)

        @pl.when((f == 0) & (i >= 1) & (i <= n_used))
        def _():
            def wait(r, carry):
                _row_copy(x_ref, xg_scr.at[slot], sems.at[slot], 0, r).wait()
                return carry
            lax.fori_loop(0, per_step * nf, wait, 0, unroll=DMA_UNROLL)

    @pl.when(active & (f == 0))
    def _():
        if gather:
            xb_scr[...] = xg_scr[slot, 0:tm, :].astype(BF16)
        else:
            xb_scr[...] = x_ref[...].astype(BF16)
        acc_scr[...] = jnp.zeros_like(acc_scr)

    @pl.when(active)
    def _():
        if gather:
            for j in range(per_step):
                r = f * per_step + j
                src_row = idxn_ref[0, 0, jnp.minimum(r, tm - 1)]
                _row_copy(x_ref, xg_scr.at[1 - slot], sems.at[1 - slot], src_row, r).start(priority=1)
        xb = xb_scr[...]
        gate = jnp.dot(xb, w1_ref[0], preferred_element_type=F32)
        up = jnp.dot(xb, w3_ref[0], preferred_element_type=F32)
        hh = (gate * _sigmoid(gate) * up).astype(BF16)
        acc_scr[...] += jnp.dot(hh, w2_ref[0], preferred_element_type=F32)

    @pl.when(active & last)
    def _():
        if final_ln:
            y = _layer_norm(alpha * x_ref[...] + acc_scr[...], g_ref[...], b_ref[...], LN_EPS)
            of_ref[...] = y
            ob_ref[...] = y.astype(BF16)
        else:
            of_ref[...] = acc_scr[...]

    @pl.when(jnp.logical_not(active) & last)
    def _():
        of_ref[...] = jnp.zeros_like(of_ref)
        if final_ln:
            ob_ref[...] = jnp.zeros_like(ob_ref)


def swiglu(x, block_expert, n_used, w1, w3, w2, g, b, alpha, final_ln, slot_rows=None, tm=512, tf=512):
    gather = slot_rows is not None
    assert not (gather and final_ln)
    d = x.shape[1]
    rows = slot_rows.shape[0] if gather else x.shape[0]
    ff = w1.shape[2]
    tm = min(tm, rows)
    nf = ff // tf
    n_blocks = rows // tm
    fi = lambda i, f, nu: jnp.where(i < nu[0], f, nf - 1)
    xs = pl.BlockSpec((tm, d), lambda i, f, be, nu: (i, 0))
    vec = pl.BlockSpec((1, d), lambda i, f, be, nu: (0, 0))
    weights = [pl.BlockSpec((1, d, tf), lambda i, f, be, nu: (be[i], 0, fi(i, f, nu))),
               pl.BlockSpec((1, d, tf), lambda i, f, be, nu: (be[i], 0, fi(i, f, nu))),
               pl.BlockSpec((1, tf, d), lambda i, f, be, nu: (be[i], fi(i, f, nu), 0)),
               vec, vec]
    scratch = [pltpu.VMEM((tm, d), BF16), pltpu.VMEM((tm, d), F32)]
    if gather:
        idx = slot_rows.reshape(n_blocks, 1, tm)
        smem_rows = lambda index_map: pl.BlockSpec((1, 1, tm), index_map, memory_space=pltpu.SMEM)
        in_specs = [smem_rows(lambda i, f, be, nu: (0, 0, 0)),
                    smem_rows(lambda i, f, be, nu: (jnp.minimum(i + 1, n_blocks - 1), 0, 0)),
                    pl.BlockSpec(memory_space=pl.ANY)] + weights
        args = (idx, idx, x)
        fetch_rows = -(-tm // nf) * nf
        scratch += [pltpu.VMEM((2, -(-fetch_rows // SUBLANES) * SUBLANES, d), F32), pltpu.SemaphoreType.DMA((2,))]
    else:
        in_specs = [pl.BlockSpec((tm, d), lambda i, f, be, nu: (jnp.minimum(i, nu[0] - 1), 0))] + weights
        args = (x,)
    n_out = 2 if final_ln else 1
    out_shape = [jax.ShapeDtypeStruct((rows, d), F32), jax.ShapeDtypeStruct((rows, d), BF16)][:n_out]
    res = pl.pallas_call(
        functools.partial(_ffn_kernel, alpha=alpha, final_ln=final_ln, gather=gather, nf=nf),
        grid_spec=pltpu.PrefetchScalarGridSpec(
            num_scalar_prefetch=2,
            grid=(n_blocks, nf),
            in_specs=in_specs,
            out_specs=[xs] * n_out,
            scratch_shapes=scratch),
        out_shape=out_shape,
        compiler_params=_params("arbitrary" if gather else "parallel", "arbitrary"),
    )(block_expert, n_used, *args, w1, w3, w2, g.reshape(1, d), b.reshape(1, d))
    return res if final_ln else res[0]


def _router_kernel(x_ref, w_ref, o_ref):
    x = x_ref[...]
    hi, mid, _ = _split3(x)
    whi, wmid, _ = _split3(w_ref[...])
    dot = lambda a, b: jnp.dot(a, b, preferred_element_type=F32)
    logits = dot(hi, whi) + (dot(hi, wmid) + dot(mid, whi))
    lane = lax.broadcasted_iota(jnp.int32, logits.shape, 1)
    neg = -jnp.inf
    lg = jnp.where(lane < N_EXPERTS, logits, neg)
    m1 = jnp.max(lg, axis=-1, keepdims=True)
    i1 = jnp.min(jnp.where(lg == m1, lane, LANES), axis=-1, keepdims=True)
    lg2 = jnp.where(lane == i1, neg, lg)
    m2 = jnp.max(lg2, axis=-1, keepdims=True)
    i2 = jnp.min(jnp.where(lg2 == m2, lane, LANES), axis=-1, keepdims=True)
    e = jnp.exp(m2 - m1)
    g1 = 1.0 / (1.0 + e)
    g2 = e / (1.0 + e)
    out = jnp.where(lane == 0, i1.astype(F32),
                    jnp.where(lane == 1, i2.astype(F32),
                              jnp.where(lane == 2, g1, jnp.where(lane == 3, g2, 0.0))))
    o_ref[...] = out


def route_top2(x, w_router_pad, tm=1024):
    t, d = x.shape
    tm = min(tm, t)
    return pl.pallas_call(
        _router_kernel,
        grid=(t // tm,),
        in_specs=[pl.BlockSpec((tm, d), lambda i: (i, 0)), pl.BlockSpec((d, LANES), lambda i: (0, 0))],
        out_specs=pl.BlockSpec((tm, LANES), lambda i: (i, 0)),
        out_shape=jax.ShapeDtypeStruct((t, LANES), F32),
        compiler_params=_params("parallel"),
    )(x, w_router_pad)


def _row_copy(src_hbm, dst_ref, sem, src_row, dst_row):
    return pltpu.make_async_copy(src_hbm.at[pl.ds(src_row, 1)], dst_ref.at[pl.ds(dst_row, 1)], sem)


def _gather_rows(idx_ref, src_hbm, dst_ref, sem, rows):
    def start(i, carry):
        for p in range(DMA_PRIORITIES):
            r = i * DMA_PRIORITIES + p
            _row_copy(src_hbm, dst_ref, sem, idx_ref[0, 0, r], r).start(priority=p)
        return carry

    def wait(r, carry):
        _row_copy(src_hbm, dst_ref, sem, 0, r).wait()
        return carry

    lax.fori_loop(0, rows // DMA_PRIORITIES, start, 0, unroll=DMA_UNROLL // DMA_PRIORITIES)
    lax.fori_loop(0, rows, wait, 0, unroll=DMA_UNROLL)


def _combine_kernel(idx0_ref, idxn_ref, y_hbm, x_ref, gate_ref, g_ref, b_ref, of_ref, ob_ref, ybuf, sems, *, alpha):
    tm = x_ref.shape[0]
    i = pl.program_id(0)
    slot = i % 2

    def wait_rows(s):
        def wait(r, carry):
            _row_copy(y_hbm, ybuf.at[s], sems.at[s], 0, r).wait()
            return carry
        lax.fori_loop(0, 2 * tm, wait, 0, unroll=DMA_UNROLL)

    @pl.when(i == 0)
    def _():
        _gather_rows(idx0_ref, y_hbm, ybuf.at[0], sems.at[0], 2 * tm)

    @pl.when(i >= 1)
    def _():
        wait_rows(slot)

    for r in range(2 * tm):
        _row_copy(y_hbm, ybuf.at[1 - slot], sems.at[1 - slot], idxn_ref[0, 0, r], r).start(
            priority=r % DMA_PRIORITIES)
    gates = gate_ref[...]
    h = gates[:, 0:1] * ybuf[slot, 0:tm, :] + gates[:, 1:2] * ybuf[slot, tm:2 * tm, :]
    y = _layer_norm(alpha * x_ref[...] + h, g_ref[...], b_ref[...], LN_EPS)
    of_ref[...] = y
    ob_ref[...] = y.astype(BF16)

    @pl.when(i == pl.num_programs(0) - 1)
    def _():
        wait_rows(1 - slot)


def moe_combine_ln(y_slots, dest, gates, x, g, b, alpha, tm=256):
    t, d = x.shape
    tm = min(tm, t)
    n_tiles = t // tm
    idx = dest.reshape(n_tiles, tm, 2).transpose(0, 2, 1).reshape(n_tiles, 1, 2 * tm)
    row = lambda i: (i, 0)
    vec = pl.BlockSpec((1, d), lambda i: (0, 0))
    smem_rows = lambda index_map: pl.BlockSpec((1, 1, 2 * tm), index_map, memory_space=pltpu.SMEM)
    return pl.pallas_call(
        functools.partial(_combine_kernel, alpha=alpha),
        grid=(n_tiles,),
        in_specs=[smem_rows(lambda i: (0, 0, 0)),
                  smem_rows(lambda i: (jnp.minimum(i + 1, n_tiles - 1), 0, 0)),
                  pl.BlockSpec(memory_space=pl.ANY),
                  pl.BlockSpec((tm, d), row), pl.BlockSpec((tm, 2), row), vec, vec],
        out_specs=[pl.BlockSpec((tm, d), row), pl.BlockSpec((tm, d), row)],
        out_shape=[jax.ShapeDtypeStruct((t, d), F32), jax.ShapeDtypeStruct((t, d), BF16)],
        scratch_shapes=[pltpu.VMEM((2, 2 * tm, d), F32), pltpu.SemaphoreType.DMA((2,))],
        compiler_params=_params("arbitrary"),
    )(idx, idx, y_slots, x, gates, g.reshape(1, d), b.reshape(1, d))


def moe_swiglu_ln(xf, w_router_pad, w1, w3, w2, g, b, alpha):
    t, d = xf.shape
    routed = route_top2(xf, w_router_pad)
    expert = routed[:, 0:2].astype(jnp.int32)
    gates = routed[:, 2:4]
    flat_e = expert.reshape(-1)
    onehot = (flat_e[:, None] == jnp.arange(N_EXPERTS, dtype=jnp.int32)[None, :]).astype(jnp.int32)
    rank = jnp.sum((jnp.cumsum(onehot, axis=0) - 1) * onehot, axis=1)
    counts = jnp.sum(onehot, axis=0)
    padded = (counts + MOE_ROWS - 1) // MOE_ROWS * MOE_ROWS
    ends = jnp.cumsum(padded)
    dest = (ends - padded)[flat_e] + rank
    n_blocks = (t * 2) // MOE_ROWS + N_EXPERTS
    n_slots = n_blocks * MOE_ROWS
    flat_tok = jnp.arange(t * 2, dtype=jnp.int32) // 2
    slot_tok = jnp.zeros((n_slots,), jnp.int32).at[dest].set(flat_tok)
    block_start = jnp.arange(n_blocks, dtype=jnp.int32) * MOE_ROWS
    block_expert = jnp.minimum(jnp.sum((ends[None, :] <= block_start[:, None]).astype(jnp.int32), axis=1),
                               N_EXPERTS - 1)
    n_used = (ends[N_EXPERTS - 1:] // MOE_ROWS).astype(jnp.int32)
    y_slots = swiglu(xf, block_expert, n_used, w1, w3, w2, g, b, alpha, final_ln=False, slot_rows=slot_tok,
                     tm=MOE_ROWS)
    return moe_combine_ln(y_slots, dest.reshape(t, 2).astype(jnp.int32), gates, xf, g, b, alpha)


def kernel(x, mem, w_in, sg_w, sg_b, sg_ln_g, sg_ln_b, hg_lb_logits, hg_norm_g, rw_mu, rw_w0, rw_w2, rw_a0, rw_a2, rw_g2, rw_k_k, rw_k_a, rw_r_k, rw_gn_g, rw_gn_b, rw_v0, rw_v1, rw_v2, w_branch, w_mix_out, xa_wq, xa_wk, xa_wv, xa_wo, ln_g, ln_b, ffn_w1, ffn_w3, ffn_w2, moe_router, moe_w1, moe_w3, moe_w2):
    batch, seq, d = x.shape
    depth = w_in.shape[0]
    t = batch * seq
    alpha = (2 * depth) ** 0.25
    w = BRANCH_W
    sg_cols, hg_cols = 2 * w, 4 * w
    rw_cols = 3 * w + RW_W_LORA + RW_A_LORA + RW_G_LORA
    o_hg, o_rw, o_gate = sg_cols, sg_cols + hg_cols, sg_cols + hg_cols + rw_cols

    p = jax.nn.softmax(hg_lb_logits.astype(F32), axis=0)
    lower_bounds = jnp.cumsum(p, axis=0) - p[0]
    lane_head = jnp.arange(LANES) // RW_HEAD
    ones_blk = (lane_head[:, None] == lane_head[None, :]).astype(BF16)

    def pad_cols(m, n):
        return jnp.pad(m, ((0, 0), (0, n - m.shape[1])))

    def pad_rows(m, n):
        return jnp.pad(m, ((0, n - m.shape[0]), (0, 0)))

    mem_b = mem.reshape(-1, d).astype(BF16)
    xf = x.reshape(t, d)
    xb = xf.astype(BF16)
    v_first = None
    for l in range(depth):
        wl_ = w_in[l]
        c0 = o_rw + 3 * w
        w_rw = jnp.concatenate([wl_[:, o_rw:c0],
                                pad_cols(wl_[:, c0:c0 + RW_W_LORA], RW_LORA_PAD),
                                pad_cols(wl_[:, c0 + RW_W_LORA:c0 + RW_W_LORA + RW_A_LORA], RW_LORA_PAD),
                                wl_[:, c0 + RW_W_LORA + RW_A_LORA:o_gate]], axis=1)
        w_proj = jnp.concatenate([wl_[:, o_hg:o_rw], w_rw, wl_[:, :o_hg]], axis=1).astype(BF16)
        w_gate = wl_[:, o_gate:].reshape(d, 3, d).transpose(1, 0, 2).astype(BF16)
        mu = rw_mu[l]
        mu_p = jnp.concatenate([mu[:3 * w],
                                jnp.pad(mu[3 * w:3 * w + RW_W_LORA], (0, RW_LORA_PAD - RW_W_LORA)),
                                jnp.pad(mu[3 * w + RW_W_LORA:3 * w + RW_W_LORA + RW_A_LORA],
                                        (0, RW_LORA_PAD - RW_A_LORA)),
                                mu[3 * w + RW_W_LORA + RW_A_LORA:]])

        zall = matmul(xb, w_proj, F32, tm=2048, tn=1024)
        y_sg = sg_branch(zall, 4, sg_w[l], sg_b[l], sg_ln_g[l], sg_ln_b[l])
        y_hg = hgrn2_branch(zall.reshape(batch, seq, -1), 0, lower_bounds[l], hg_norm_g[l]).reshape(t, w)
        vmix = None if l == 0 else (rw_v0[l - 1], rw_v1[l - 1].astype(BF16), rw_v2[l - 1].astype(BF16))
        r_, lw_, k_, v_, kk_, al_, g_ = rw_prep(
            zall, 1, seq, mu_p, rw_w0[l], pad_rows(rw_w2[l], RW_LORA_PAD).astype(BF16), rw_a0[l],
            pad_rows(rw_a2[l], RW_LORA_PAD).astype(BF16), rw_g2[l].astype(BF16), rw_k_k[l], rw_k_a[l],
            ones_blk, vmix, v_first)
        if l == 0:
            v_first = v_
        y_rw = rwkv7_mix(r_, lw_, k_, v_, kk_, al_, g_, rw_gn_g[l], rw_gn_b[l], rw_r_k[l], ones_blk,
                         batch, seq).reshape(t, w)
        merged = merge_branches(xb, y_sg, y_hg, y_rw, w_gate, w_branch[l].astype(BF16))
        xf, xb = matmul_res_ln(merged, w_mix_out[l].astype(BF16), xf, ln_g[l, 0], ln_b[l, 0], alpha)

        kmem = matmul(mem_b, xa_wk[l].astype(BF16), BF16).reshape(batch, -1, XA_HEADS * XA_HEAD_DIM)
        vmem = matmul(mem_b, xa_wv[l].astype(BF16), BF16).reshape(batch, -1, XA_HEADS * XA_HEAD_DIM)
        xf3, xb3 = cross_attention_ln(xb.reshape(batch, seq, d), xf.reshape(batch, seq, d), kmem, vmem,
                                      xa_wq[l].astype(BF16), xa_wo[l].astype(BF16), ln_g[l, 1], ln_b[l, 1], alpha)
        xf, xb = xf3.reshape(t, d), xb3.reshape(t, d)

        if l % 2 == 0:
            i = l // 2
            n_row_blocks = t // min(512, t)
            xf, xb = swiglu(xf, jnp.zeros((n_row_blocks,), jnp.int32), jnp.full((1,), n_row_blocks, jnp.int32),
                            ffn_w1[i:i + 1].astype(BF16),
                            ffn_w3[i:i + 1].astype(BF16), ffn_w2[i:i + 1].astype(BF16),
                            ln_g[l, 2], ln_b[l, 2], alpha, final_ln=True)
        else:
            i = l // 2
            xf, xb = moe_swiglu_ln(xf, pad_cols(moe_router[i], LANES), moe_w1[i].astype(BF16),
                                   moe_w3[i].astype(BF16), moe_w2[i].astype(BF16), ln_g[l, 2], ln_b[l, 2], alpha)
    return xf.reshape(batch, seq, d)
```
